```python
import math
import jax, jax.numpy as jnp
from jax import lax
import numpy as np


D_MODEL = 1024
BATCH = 4
SEQ = 8192
DEPTH = 4

GRID_W = 64
CTX_LEN = 256
N_MIXERS = 3
N_LAYERS_A = (DEPTH + 2) // 3
N_LAYERS_B = (DEPTH + 1) // 3
N_LAYERS_C = DEPTH // 3
EPS = 1e-6
ROPE_BASE = 10000.0
Q_BLOCK = 128
N_MOD = 6

MLA_HEADS = 16
MLA_Q_RANK = 256
MLA_KV_RANK = 256
MLA_NOPE = 64
MLA_ROPE = 32
MLA_V = 64

DIFF_HEAD_DIM = 64
DIFF_HEADS = D_MODEL // (2 * DIFF_HEAD_DIM)

GLA_HEADS = 4
GLA_DK = D_MODEL // 2
GLA_DV = D_MODEL
GLA_GATE_RANK = 16
GLA_GATE_NORM = 16.0
GLA_CHUNK = 64

MOE_GROUPS = 4
MOE_EXPERTS = 4
MOE_TOP_K = 2
MOE_D_EXPERT = 512

kernel_name = 'hybrid_mla_diff_gla_hmoe_prefix_dit'


def rms_norm(x, gain):
    xf = x.astype(jnp.float32)
    xf = xf * lax.rsqrt(jnp.mean(xf * xf, axis=-1, keepdims=True) + EPS)
    return xf.astype(x.dtype) * gain


def modulate(h, shift, scale):
    return h * (1 + scale) + shift


def split_heads(t, n_heads):
    b, n, _ = t.shape
    return t.reshape(b, n, n_heads, -1).transpose(0, 2, 1, 3)


def merge_heads(t):
    b, h, n, d = t.shape
    return t.transpose(0, 2, 1, 3).reshape(b, n, h * d)


def axial_rope_tables(n_rows, rot_dim, dtype):
    row = jnp.repeat(jnp.arange(n_rows, dtype=jnp.float32), GRID_W)
    col = jnp.tile(jnp.arange(GRID_W, dtype=jnp.float32), n_rows)
    n_freq = rot_dim // 4
    inv_freq = ROPE_BASE ** (-jnp.arange(n_freq, dtype=jnp.float32) / n_freq)
    ang = jnp.concatenate([row[:, None] * inv_freq, col[:, None] * inv_freq], axis=-1)
    return jnp.cos(ang).astype(dtype), jnp.sin(ang).astype(dtype)


def apply_rope(t, cos, sin):
    t1, t2 = jnp.split(t, 2, axis=-1)
    return jnp.concatenate([t1 * cos - t2 * sin, t1 * sin + t2 * cos], axis=-1)


def to_blocks(t):
    b, h, n = t.shape[:3]
    t = t.reshape(b, h, n // Q_BLOCK, Q_BLOCK, *t.shape[3:])
    return jnp.moveaxis(t, 2, 0)


def from_blocks(t):
    t = jnp.moveaxis(t, 0, 2)
    return t.reshape(t.shape[0], t.shape[1], -1, t.shape[-1])


def mla_mixer(hc, hl, cos, sin, w_dqkv, q_norm, w_uq, kv_norm, w_ukv, w_o, ctx_out):
    scale = (MLA_NOPE + MLA_ROPE) ** -0.5

    def down(h):
        return jnp.split(h @ w_dqkv, [MLA_Q_RANK, MLA_Q_RANK + MLA_KV_RANK], axis=-1)

    def queries(cq):
        q = split_heads(rms_norm(cq, q_norm) @ w_uq, MLA_HEADS)
        return jnp.split(q, [MLA_NOPE], axis=-1)

    def keys_values(ckv):
        kv = split_heads(rms_norm(ckv, kv_norm) @ w_ukv, MLA_HEADS)
        return jnp.split(kv, [MLA_NOPE], axis=-1)

    def attend(q_nope, q_pe, k_nope, k_pe, v):
        s = jnp.einsum('bhqd,bhkd->bhqk', q_nope, k_nope) + jnp.einsum('bhqr,bkr->bhqk', q_pe, k_pe)
        p = jax.nn.softmax(s.astype(jnp.float32) * scale, axis=-1).astype(v.dtype)
        return jnp.einsum('bhqk,bhkd->bhqd', p, v)

    cq_c, ckv_c, kpe_c = down(hc)
    cq_l, ckv_l, kpe_l = down(hl)
    kn_c, v_c = keys_values(ckv_c)
    kn_l, v_l = keys_values(ckv_l)
    qn_l, qpe_l = queries(cq_l)
    qpe_l = apply_rope(qpe_l, cos, sin)
    kpe_l = apply_rope(kpe_l, cos, sin)
    kn_all = jnp.concatenate([kn_c, kn_l], axis=2)
    kpe_all = jnp.concatenate([kpe_c, kpe_l], axis=1)
    v_all = jnp.concatenate([v_c, v_l], axis=2)
    o_l = from_blocks(lax.map(lambda qb: attend(qb[0], qb[1], kn_all, kpe_all, v_all),
                              (to_blocks(qn_l), to_blocks(qpe_l))))
    y_l = merge_heads(o_l) @ w_o
    if not ctx_out:
        return None, y_l
    qn_c, qpe_c = queries(cq_c)
    y_c = merge_heads(attend(qn_c, qpe_c, kn_c, kpe_c, v_c)) @ w_o
    return y_c, y_l


def diff_mixer(hc, hl, cos, sin, w_qkv, lam, subln, w_o, lam_init, ctx_out):
    scale = DIFF_HEAD_DIM ** -0.5
    lam_full = (jnp.exp(jnp.sum(lam[0] * lam[1], dtype=jnp.float32))
                - jnp.exp(jnp.sum(lam[2] * lam[3], dtype=jnp.float32)) + lam_init)

    def project(h):
        b, n, _ = h.shape
        q, k, v = jnp.split(h @ w_qkv, 3, axis=-1)
        q = q.reshape(b, n, DIFF_HEADS, 2, DIFF_HEAD_DIM).transpose(0, 2, 1, 3, 4)
        k = k.reshape(b, n, DIFF_HEADS, 2, DIFF_HEAD_DIM).transpose(0, 2, 1, 3, 4)
        return q, k, split_heads(v, DIFF_HEADS)

    def attend(q, k, v):
        s = jnp.einsum('bhqsd,bhksd->bhsqk', q, k)
        p = jax.nn.softmax(s.astype(jnp.float32) * scale, axis=-1)
        a = (p[:, :, 0] - lam_full * p[:, :, 1]).astype(v.dtype)
        return jnp.einsum('bhqk,bhkd->bhqd', a, v)

    def out(o):
        return merge_heads(rms_norm(o, subln) * (1.0 - lam_init)) @ w_o

    q_c, k_c, v_c = project(hc)
    q_l, k_l, v_l = project(hl)
    rc, rs = cos[:, None, :], sin[:, None, :]
    q_l = apply_rope(q_l, rc, rs)
    k_l = apply_rope(k_l, rc, rs)
    k_all = jnp.concatenate([k_c, k_l], axis=2)
    v_all = jnp.concatenate([v_c, v_l], axis=2)
    o_l = from_blocks(lax.map(lambda qb: attend(qb, k_all, v_all), to_blocks(q_l)))
    y_l = out(o_l)
    if not ctx_out:
        return None, y_l
    return out(attend(q_c, k_c, v_c)), y_l


def gla_scan_chunked(q, k, v, log_a, s0, with_output):
    b, h, n, dk = q.shape
    dv = v.shape[-1]
    nc = n // GLA_CHUNK
    chunk = lambda t: t.reshape(b, h, nc, GLA_CHUNK, t.shape[-1])
    qc, kc, vc = chunk(q), chunk(k), chunk(v)
    cum = jnp.cumsum(chunk(log_a), axis=3)
    cum_last = cum[:, :, :, -1:, :]
    chunk_kv = jnp.einsum('bhcsk,bhcsv->bhckv', kc * jnp.exp(cum_last - cum), vc)
    decay = jnp.exp(cum_last[:, :, :, 0, :])

    def step(state, inp):
        dec, kv = inp
        return dec[..., None] * state + kv, (state if with_output else None)

    s_final, s_start = lax.scan(step, s0, (jnp.moveaxis(decay, 2, 0), jnp.moveaxis(chunk_kv, 2, 0)))
    if not with_output:
        return None, s_final
    s_start = jnp.moveaxis(s_start, 0, 2)
    q_dec = qc * jnp.exp(cum)
    k_inv = kc * jnp.exp(-cum)
    lower_tri = jnp.tril(jnp.ones((GLA_CHUNK, GLA_CHUNK), dtype=bool))
    att = jnp.where(lower_tri, jnp.einsum('bhcqk,bhcsk->bhcqs', q_dec, k_inv), 0.0)
    o = (jnp.einsum('bhcqk,bhckv->bhcqv', q_dec, s_start)
         + jnp.einsum('bhcqs,bhcsv->bhcqv', att, vc))
    return o.reshape(b, h, n, dv), s_final


def gla_mixer(hc, hl, w_in, w_gate_up, b_gate, head_norm, w_o, ctx_out):
    splits = [GLA_DK, 2 * GLA_DK, 2 * GLA_DK + GLA_DV, 2 * GLA_DK + 2 * GLA_DV,
              2 * GLA_DK + 2 * GLA_DV + GLA_GATE_RANK]
    dk_h, dv_h = GLA_DK // GLA_HEADS, GLA_DV // GLA_HEADS

    def log_decay(z, d):
        a = jax.nn.log_sigmoid((z @ w_gate_up[d] + b_gate[d]).astype(jnp.float32)) / GLA_GATE_NORM
        return split_heads(a, GLA_HEADS)

    def project(h):
        q, k, v, g, zf, zb = jnp.split(h @ w_in, splits, axis=-1)
        q = split_heads(q, GLA_HEADS) * dk_h ** -0.5
        return (q, split_heads(k, GLA_HEADS), split_heads(v, GLA_HEADS), g,
                log_decay(zf, 0), log_decay(zb, 1))

    flip = lambda t: jnp.flip(t, axis=2)

    def out(o, g):
        return (merge_heads(rms_norm(o.astype(g.dtype), head_norm)) * jax.nn.silu(g)) @ w_o

    q_c, k_c, v_c, g_c, af_c, ab_c = project(hc)
    q_l, k_l, v_l, g_l, af_l, ab_l = project(hl)
    zeros = jnp.zeros((hc.shape[0], GLA_HEADS, dk_h, dv_h), jnp.float32)
    o_cf, s_f = gla_scan_chunked(q_c, k_c, v_c, af_c, zeros, ctx_out)
    o_cb, s_b = gla_scan_chunked(flip(q_c), flip(k_c), flip(v_c), flip(ab_c), zeros, ctx_out)
    o_lf, _ = gla_scan_chunked(q_l, k_l, v_l, af_l, s_f, True)
    o_lb, _ = gla_scan_chunked(flip(q_l), flip(k_l), flip(v_l), flip(ab_l), s_b, True)
    y_l = out(o_lf + flip(o_lb), g_l)
    if not ctx_out:
        return None, y_l
    return out(o_cf + flip(o_cb), g_c), y_l


def hier_moe(h, w_group, b_group, w_router, b_router, w_gate, w_up, w_down):
    grp_logits = (h @ w_group + b_group).astype(jnp.float32)
    grp_prob = jax.nn.softmax(grp_logits, axis=-1)
    grp_onehot = jax.nn.one_hot(jnp.argmax(grp_logits, axis=-1), MOE_GROUPS, dtype=jnp.float32)
    grp_w = jnp.sum(grp_prob * grp_onehot, axis=-1, keepdims=True)
    exp_logits = jnp.einsum('bnd,gde->bnge', h, w_router) + b_router
    exp_logits = jnp.einsum('bnge,bng->bne', exp_logits.astype(jnp.float32), grp_onehot)
    top_val, top_idx = lax.top_k(exp_logits, MOE_TOP_K)
    top_w = jax.nn.softmax(top_val, axis=-1) * grp_w
    exp_gate = jnp.einsum('bnk,bnke->bne', top_w, jax.nn.one_hot(top_idx, MOE_EXPERTS, dtype=jnp.float32))
    gate = (grp_onehot[..., None] * exp_gate[..., None, :]).astype(h.dtype)
    y = jnp.zeros_like(h)
    for gi in range(MOE_GROUPS):
        a = jnp.einsum('bnd,edf->bnef', h, w_gate[gi])
        u = jnp.einsum('bnd,edf->bnef', h, w_up[gi])
        y = y + jnp.einsum('bnef,efd->bnd', jax.nn.silu(a) * u * gate[:, :, gi, :, None], w_down[gi])
    return y


def setup_inputs(seed: int = 0) -> dict:
    key = jax.random.key(seed)
    ks = iter(jax.random.split(key, 40))
    f32 = jnp.float32
    D = D_MODEL

    def w(shape, fan_in, mult=1.0):
        return jax.random.normal(next(ks), shape, f32) * (mult * fan_in ** -0.5)

    def gain(shape):
        return 1.0 + 0.05 * jax.random.normal(next(ks), shape, f32)

    def bias(shape, s=0.02):
        return s * jax.random.normal(next(ks), shape, f32)

    return {
        'x': jax.random.normal(next(ks), (BATCH, SEQ, D), f32),
        'c': jax.random.normal(next(ks), (BATCH, D), f32),
        'ctx': jax.random.normal(next(ks), (BATCH, CTX_LEN, D), f32),
        'c_ctx': jax.random.normal(next(ks), (D,), f32),
        'w_ada': w((DEPTH, D, N_MOD * D), D, 0.5),
        'b_ada': bias((DEPTH, N_MOD * D)),
        'norm_mix': gain((DEPTH, D)),
        'norm_ffn': gain((DEPTH, D)),
        'mla_w_dqkv': w((N_LAYERS_A, D, MLA_Q_RANK + MLA_KV_RANK + MLA_ROPE), D),
        'mla_q_norm': gain((N_LAYERS_A, MLA_Q_RANK)),
        'mla_w_uq': w((N_LAYERS_A, MLA_Q_RANK, MLA_HEADS * (MLA_NOPE + MLA_ROPE)), MLA_Q_RANK),
        'mla_kv_norm': gain((N_LAYERS_A, MLA_KV_RANK)),
        'mla_w_ukv': w((N_LAYERS_A, MLA_KV_RANK, MLA_HEADS * (MLA_NOPE + MLA_V)), MLA_KV_RANK),
        'mla_w_o': w((N_LAYERS_A, MLA_HEADS * MLA_V, D), MLA_HEADS * MLA_V),
        'diff_w_qkv': w((N_LAYERS_B, D, 3 * D), D),
        'diff_lambda': bias((N_LAYERS_B, 4, DIFF_HEAD_DIM), 0.1),
        'diff_subln': gain((N_LAYERS_B, 2 * DIFF_HEAD_DIM)),
        'diff_w_o': w((N_LAYERS_B, D, D), D),
        'gla_w_in': w((N_LAYERS_C, D, 2 * GLA_DK + 2 * GLA_DV + 2 * GLA_GATE_RANK), D),
        'gla_w_gate_up': w((N_LAYERS_C, 2, GLA_GATE_RANK, GLA_DK), GLA_GATE_RANK),
        'gla_b_gate': bias((N_LAYERS_C, 2, GLA_DK), 0.1),
        'gla_head_norm': gain((N_LAYERS_C, GLA_DV // GLA_HEADS)),
        'gla_w_o': w((N_LAYERS_C, GLA_DV, D), GLA_DV),
        'moe_w_group': w((DEPTH, D, MOE_GROUPS), D),
        'moe_b_group': bias((DEPTH, MOE_GROUPS), 0.01),
        'moe_w_router': w((DEPTH, MOE_GROUPS, D, MOE_EXPERTS), D),
        'moe_b_router': bias((DEPTH, MOE_GROUPS, MOE_EXPERTS), 0.01),
        'moe_w_gate': w((DEPTH, MOE_GROUPS, MOE_EXPERTS, D, MOE_D_EXPERT), D),
        'moe_w_up': w((DEPTH, MOE_GROUPS, MOE_EXPERTS, D, MOE_D_EXPERT), D),
        'moe_w_down': w((DEPTH, MOE_GROUPS, MOE_EXPERTS, MOE_D_EXPERT, D), MOE_D_EXPERT),
        'final_norm': gain((D,)),
    }


def reference(x, c, ctx, c_ctx, w_ada, b_ada, norm_mix, norm_ffn,
              mla_w_dqkv, mla_q_norm, mla_w_uq, mla_kv_norm, mla_w_ukv, mla_w_o,
              diff_w_qkv, diff_lambda, diff_subln, diff_w_o,
              gla_w_in, gla_w_gate_up, gla_b_gate, gla_head_norm, gla_w_o,
              moe_w_group, moe_b_group, moe_w_router, moe_b_router,
              moe_w_gate, moe_w_up, moe_w_down, final_norm):
    n_tok = x.shape[1]
    ROWS = n_tok // GRID_W
    cos_a, sin_a = axial_rope_tables(ROWS, MLA_ROPE, x.dtype)
    cos_b, sin_b = axial_rope_tables(ROWS, DIFF_HEAD_DIM, x.dtype)
    n_ctx = ctx.shape[1]
    xl, xc = x, ctx
    for li in range(DEPTH):
        last = li == DEPTH - 1
        mod_l = jnp.split((jax.nn.silu(c) @ w_ada[li] + b_ada[li])[:, None, :], N_MOD, axis=-1)
        mod_c = jnp.split(jax.nn.silu(c_ctx) @ w_ada[li] + b_ada[li], N_MOD, axis=-1)
        hl = modulate(rms_norm(xl, norm_mix[li]), mod_l[0], mod_l[1])
        hc = modulate(rms_norm(xc, norm_mix[li]), mod_c[0], mod_c[1])
        kind, j = li % N_MIXERS, li // N_MIXERS
        if kind == 0:
            yc, yl = mla_mixer(hc, hl, cos_a, sin_a, mla_w_dqkv[j], mla_q_norm[j], mla_w_uq[j],
                               mla_kv_norm[j], mla_w_ukv[j], mla_w_o[j], not last)
        elif kind == 1:
            lam_init = 0.8 - 0.6 * math.exp(-0.3 * li)
            yc, yl = diff_mixer(hc, hl, cos_b, sin_b, diff_w_qkv[j], diff_lambda[j], diff_subln[j],
                                diff_w_o[j], lam_init, not last)
        else:
            yc, yl = gla_mixer(hc, hl, gla_w_in[j], gla_w_gate_up[j], gla_b_gate[j],
                               gla_head_norm[j], gla_w_o[j], not last)
        xl = xl + mod_l[2] * yl
        moe = lambda h: hier_moe(h, moe_w_group[li], moe_b_group[li], moe_w_router[li], moe_b_router[li],
                                 moe_w_gate[li], moe_w_up[li], moe_w_down[li])
        hl = modulate(rms_norm(xl, norm_ffn[li]), mod_l[3], mod_l[4])
        if last:
            xl = xl + mod_l[5] * moe(hl)
        else:
            xc = xc + mod_c[2] * yc
            hc = modulate(rms_norm(xc, norm_ffn[li]), mod_c[3], mod_c[4])
            y = moe(jnp.concatenate([hc, hl], axis=1))
            xc = xc + mod_c[5] * y[:, :n_ctx]
            xl = xl + mod_l[5] * y[:, n_ctx:]
    return rms_norm(xl, final_norm)
```

```python
import functools
import math

import jax
import jax.numpy as jnp
from jax import lax
from jax.experimental import pallas as pl
from jax.experimental.pallas import tpu as pltpu

F32 = jnp.float32
BF16 = jnp.bfloat16

EPS = 1e-6
ROPE_BASE = 10000.0
GRID_W = 64
N_MOD = 6
LANES = 128

MLA_HEADS = 16
MLA_Q_RANK = 256
MLA_KV_RANK = 256
MLA_NOPE = 64
MLA_ROPE = 32
MLA_V = 64
DIFF_HEAD_DIM = 64
GLA_HEADS = 4
GLA_GATE_RANK = 16
GLA_GATE_NORM = 16.0
GLA_CHUNK = 128
MOE_GROUPS = 4
MOE_EXPERTS = 4

ROW_TILE = 256
MOE_TILE = 1024
VMEM_LIMIT = 52 * 1024 * 1024


def _cp(*sem):
    return pltpu.CompilerParams(dimension_semantics=sem, vmem_limit_bytes=VMEM_LIMIT)


def _mm(a, b):
    return jnp.dot(a, b, preferred_element_type=F32)


def _nt(a, b):
    return lax.dot_general(a, b, (((1,), (1,)), ((), ())), preferred_element_type=F32)


def _rms(x):
    return x * lax.rsqrt(jnp.mean(x * x, axis=-1, keepdims=True) + EPS)


def _silu(x):
    return x * (1.0 / (1.0 + jnp.exp(-x)))


def _ada_body(c_ref, w_ref, b_ref, o_ref):
    c = c_ref[...]
    o_ref[0] = _mm(_silu(c).astype(BF16), w_ref[0].astype(BF16)) + b_ref[0]


def _ada_all(cvec, w_ada, b_ada):
    depth, d, n6 = w_ada.shape
    tn = 1536
    return pl.pallas_call(
        _ada_body,
        grid=(depth, n6 // tn),
        in_specs=[pl.BlockSpec((8, d), lambda l, j: (0, 0)),
                  pl.BlockSpec((1, d, tn), lambda l, j: (l, 0, j)),
                  pl.BlockSpec((1, 1, tn), lambda l, j: (l, 0, j))],
        out_specs=pl.BlockSpec((1, 8, tn), lambda l, j: (l, 0, j)),
        out_shape=jax.ShapeDtypeStruct((depth, 8, n6), F32),
        compiler_params=_cp("parallel", "parallel"),
        name="adaln",
    )(cvec, w_ada, b_ada.reshape(depth, 1, n6))


class _Rows:
    def __init__(self, B, T, n_lat, tm=ROW_TILE):
        self.B, self.T, self.tm = B, T, tm
        self.n_lat_tiles = n_lat // tm
        self.grid = (B, T // tm)

    def rows(self, c):
        return pl.BlockSpec((1, self.tm, c), lambda b, t: (b, t, 0))

    def mod(self, d):
        nl, B = self.n_lat_tiles, self.B
        return pl.BlockSpec((1, N_MOD, d), lambda b, t: (jnp.where(t < nl, b, B), 0, 0))

    def table(self, c):
        return pl.BlockSpec((self.tm, c), lambda b, t: (t, 0))

    @staticmethod
    def full(a):
        nd = a.ndim
        return pl.BlockSpec(a.shape, lambda b, t: (0,) * nd)

    def out(self, c, dtype):
        return jax.ShapeDtypeStruct((self.B, self.T, c), dtype)


def _prologue(refs, has_prev, k_shift):
    if has_prev:
        x_ref, y_ref, modp_ref, mod_ref, gain_ref = refs[:5]
        x = x_ref[0] + modp_ref[0, 5:6, :] * y_ref[0]
        rest = refs[5:]
    else:
        x_ref, mod_ref, gain_ref = refs[:3]
        x = x_ref[0]
        rest = refs[3:]
    h = _rms(x) * gain_ref[...]
    h = h * (1.0 + mod_ref[0, k_shift + 1:k_shift + 2, :]) + mod_ref[0, k_shift:k_shift + 1, :]
    return x, h, rest


def _mla_proj_body(has_prev, *refs):
    x, h, rest = _prologue(refs, has_prev, 0)
    (wd_ref, qn_ref, kvn_ref, wq_ref, wqs_ref, wk_ref, wv_ref, c_ref, s_ref), outs = rest[:9], rest[9:]
    if has_prev:
        x1_ref, q_ref, k_ref, v_ref = outs
        x1_ref[0] = x
    else:
        q_ref, k_ref, v_ref = outs
    d = _mm(h.astype(BF16), wd_ref[...])
    cos, sin = c_ref[...], s_ref[...]
    cq = (_rms(d[:, :MLA_Q_RANK]) * qn_ref[...]).astype(BF16)
    ckv = (_rms(d[:, MLA_Q_RANK:MLA_Q_RANK + MLA_KV_RANK]) * kvn_ref[...]).astype(BF16)
    o = MLA_Q_RANK + MLA_KV_RANK
    pe = d[:, o:o + LANES] * cos + d[:, o + LANES:o + 2 * LANES] * sin
    qa = _mm(cq, wq_ref[...])
    qb = _mm(cq, wqs_ref[...])
    kn = _mm(ckv, wk_ref[...])
    for hd in range(MLA_HEADS):
        sl = slice(hd * LANES, (hd + 1) * LANES)
        q_ref[0, :, sl] = (qa[:, sl] * cos + qb[:, sl] * sin).astype(BF16)
        k_ref[0, :, sl] = (kn[:, sl] + pe).astype(BF16)
    v_ref[0] = _mm(ckv, wv_ref[...]).astype(BF16)


def _diff_proj_body(has_prev, *refs):
    x, h, rest = _prologue(refs, has_prev, 0)
    (w_ref, cq_ref, sq_ref, ck_ref, sk_ref), outs = rest[:5], rest[5:]
    if has_prev:
        x1_ref, q_ref, k_ref, v_ref = outs
        x1_ref[0] = x
    else:
        q_ref, k_ref, v_ref = outs
    dm = h.shape[-1]
    p = _mm(h.astype(BF16), w_ref[...])
    cq, sq, ck, sk = cq_ref[...], sq_ref[...], ck_ref[...], sk_ref[...]
    for hd in range(dm // LANES):
        sl = slice(hd * LANES, (hd + 1) * LANES)
        q_ref[0, :, sl] = (p[:, sl] * cq + p[:, 3 * dm + hd * LANES:3 * dm + (hd + 1) * LANES] * sq).astype(BF16)
        k_ref[0, :, sl] = (p[:, dm + hd * LANES:dm + (hd + 1) * LANES] * ck
                           + p[:, 4 * dm + hd * LANES:4 * dm + (hd + 1) * LANES] * sk).astype(BF16)
    v_ref[0] = p[:, 2 * dm:3 * dm].astype(BF16)


def _gla_proj_body(has_prev, *refs):
    x, h, rest = _prologue(refs, has_prev, 0)
    (w_ref, wvt_ref, wgu_ref, bg_ref), outs = rest[:4], rest[4:]
    if has_prev:
        x1_ref, q_ref, k_ref, v_ref, vt_ref, g_ref, af_ref, ab_ref = outs
        x1_ref[0] = x
    else:
        q_ref, k_ref, v_ref, vt_ref, g_ref, af_ref, ab_ref = outs
    dm = h.shape[-1]
    dk = dm // 2
    hb = h.astype(BF16)
    p = _mm(hb, w_ref[...])
    q_ref[0] = p[:, :dk] * ((dk // GLA_HEADS) ** -0.5)
    k_ref[0] = p[:, dk:2 * dk]
    v_ref[0] = p[:, 2 * dk:2 * dk + dm].astype(BF16)
    vt_ref[0] = _nt(wvt_ref[...], hb).astype(BF16)
    g_ref[0] = p[:, 2 * dk + dm:2 * dk + 2 * dm]
    z = p[:, 2 * dk + 2 * dm:].astype(BF16)
    a = _mm(z, wgu_ref[...]) + bg_ref[...]
    ls = (jnp.minimum(a, 0.0) - jnp.log(1.0 + jnp.exp(-jnp.abs(a)))) / GLA_GATE_NORM
    af_ref[0] = ls[:, :dk]
    ab_ref[0] = ls[:, dk:]


def _route(logits):
    lane = lax.broadcasted_iota(jnp.int32, logits.shape, 1)

    def col(i):
        return jnp.sum(jnp.where(lane == i, logits, 0.0), axis=-1, keepdims=True)

    lg = [col(i) for i in range(MOE_GROUPS)]
    gmax = functools.reduce(jnp.maximum, lg)
    onehot, taken = [], None
    for i in range(MOE_GROUPS):
        hit = lg[i] == gmax
        if taken is not None:
            hit = jnp.logical_and(hit, jnp.logical_not(taken))
        taken = hit if taken is None else jnp.logical_or(taken, hit)
        onehot.append(hit)
    grp_w = 1.0 / functools.reduce(lambda a, b: a + b, [jnp.exp(l - gmax) for l in lg])
    le = []
    for e in range(MOE_EXPERTS):
        v = None
        for g in range(MOE_GROUPS):
            c = jnp.where(onehot[g], col(MOE_GROUPS + g * MOE_EXPERTS + e), 0.0)
            v = c if v is None else v + c
        le.append(v)
    vmax = functools.reduce(jnp.maximum, le)
    num = []
    for e in range(MOE_EXPERTS):
        rank = None
        for o in range(MOE_EXPERTS):
            if o == e:
                continue
            ahead = (le[o] >= le[e]) if o < e else (le[o] > le[e])
            r = jnp.where(ahead, 1.0, 0.0)
            rank = r if rank is None else rank + r
        num.append(jnp.where(rank < 2.0, jnp.exp(le[e] - vmax), 0.0))
    scale = grp_w / functools.reduce(lambda a, b: a + b, num)
    gates = jnp.zeros(logits.shape, F32)
    for g in range(MOE_GROUPS):
        for e in range(MOE_EXPERTS):
            val = jnp.where(onehot[g], num[e] * scale, 0.0)
            gates = jnp.where(lane == g * MOE_EXPERTS + e, val, gates)
    return gates


def _post_body(kind, *refs):
    x1_ref = refs[0]
    if kind == "gla":
        of_ref, ob_ref, g_ref, hn_ref = refs[1:5]
        rest = refs[5:]
        o = of_ref[0] + ob_ref[0]
        hn = hn_ref[...]
        dv = hn.shape[-1]
        parts = [_rms(o[:, i * dv:(i + 1) * dv]) * hn for i in range(o.shape[-1] // dv)]
        a = (jnp.concatenate(parts, axis=-1) * _silu(g_ref[0])).astype(BF16)
    else:
        a = refs[1][0]
        rest = refs[2:]
    mod_ref, gain_ref, wo_ref, wr_ref, br_ref, x2_ref, h2_ref, gate_ref = rest
    x2 = x1_ref[0] + mod_ref[0, 2:3, :] * _mm(a, wo_ref[...])
    x2_ref[0] = x2
    h2 = _rms(x2) * gain_ref[...]
    h2 = (h2 * (1.0 + mod_ref[0, 4:5, :]) + mod_ref[0, 3:4, :]).astype(BF16)
    h2_ref[0] = h2
    gate_ref[0] = _route(_mm(h2, wr_ref[...]) + br_ref[...])


def _final_body(x_ref, y_ref, mod_ref, gain_ref, o_ref):
    x = x_ref[0] + mod_ref[0, 5:6, :] * y_ref[0]
    o_ref[0] = _rms(x) * gain_ref[...]


def _softmax_first(s, c):
    m = jnp.max(s, axis=-1, keepdims=True)
    p = jnp.exp2((s - m) * c)
    return m, jnp.sum(p, axis=-1, keepdims=True), p.astype(BF16)


def _attn_body(mode, n_lat, n_ctx, tk, c, lam_init, *refs):
    if mode == "mla":
        q_ref, k_ref, v_ref, o_ref, m_sc, l_sc, acc_sc = refs
    else:
        q_ref, k_ref, v_ref, lam_ref, sub_ref, o_ref, m_sc, l_sc, acc_sc = refs
    tq = q_ref.shape[1]
    is_lat = pl.program_id(2) < (n_lat // tq)
    lane = lax.broadcasted_iota(jnp.int32, (tq, LANES), 1)

    def q_of(i):
        if mode == "mla":
            return q_ref[0, :, i * LANES:(i + 1) * LANES]
        q = q_ref[0]
        keep = (lane < DIFF_HEAD_DIM) if i == 0 else (lane >= DIFF_HEAD_DIM)
        return jnp.where(keep, q, jnp.zeros_like(q))

    def k_of(i, rows):
        if mode == "mla":
            return k_ref[0, rows, i * LANES:(i + 1) * LANES]
        return k_ref[0, rows, :]

    ctx_rows = slice(n_lat, n_lat + n_ctx)
    for i in range(2):
        m, l, p = _softmax_first(_nt(q_of(i), k_of(i, ctx_rows)), c)
        m_sc[i] = m
        l_sc[i] = l
        acc_sc[i] = _mm(p, v_ref[0, ctx_rows, :])

    @pl.when(is_lat)
    def _():
        for i in range(2):
            q = q_of(i)

            def step(j, carry, q=q, i=i):
                rows = pl.ds(pl.multiple_of(j * tk, tk), tk)
                s = _nt(q, k_of(i, rows))
                m_old = m_sc[i]
                m_new = jnp.maximum(m_old, jnp.max(s, axis=-1, keepdims=True))
                alpha = jnp.exp2((m_old - m_new) * c)
                p = jnp.exp2((s - m_new) * c)
                l_sc[i] = alpha * l_sc[i] + jnp.sum(p, axis=-1, keepdims=True)
                acc_sc[i] = alpha * acc_sc[i] + _mm(p.astype(BF16), v_ref[0, rows, :])
                m_sc[i] = m_new
                return carry

            lax.fori_loop(0, n_lat // tk, step, 0)

    o0 = acc_sc[0] * (1.0 / l_sc[0])
    o1 = acc_sc[1] * (1.0 / l_sc[1])
    if mode == "mla":
        o_ref[0] = jnp.where(lane < MLA_V, o0, o1).astype(BF16)
    else:
        lam = lam_ref[...]
        lam_full = (jnp.exp(jnp.sum(lam[0:1] * lam[1:2], axis=-1, keepdims=True))
                    - jnp.exp(jnp.sum(lam[2:3] * lam[3:4], axis=-1, keepdims=True)) + lam_init)
        o = o0 - lam_full * o1
        o_ref[0] = ((_rms(o) * sub_ref[...]) * (1.0 - lam_init)).astype(BF16)


def _attention(mode, q, k, v, n_lat, n_ctx, c, extra=(), lam_init=0.0):
    B, T, _ = q.shape
    tq = ROW_TILE
    tk = math.gcd(n_lat, 512)
    qw = 2 * LANES if mode == "mla" else LANES
    nh = q.shape[-1] // qw
    in_specs = [pl.BlockSpec((1, tq, qw), lambda b, h, i: (b, i, h)),
                pl.BlockSpec((1, T, qw), lambda b, h, i: (b, 0, h)),
                pl.BlockSpec((1, T, LANES), lambda b, h, i: (b, 0, h))]
    for a in extra:
        in_specs.append(pl.BlockSpec(a.shape, lambda b, h, i, nd=a.ndim: (0,) * nd))
    return pl.pallas_call(
        functools.partial(_attn_body, mode, n_lat, n_ctx, tk, c, lam_init),
        grid=(B, nh, T // tq),
        in_specs=in_specs,
        out_specs=pl.BlockSpec((1, tq, LANES), lambda b, h, i: (b, i, h)),
        out_shape=jax.ShapeDtypeStruct((B, T, nh * LANES), BF16),
        scratch_shapes=[pltpu.VMEM((2, tq, 1), F32), pltpu.VMEM((2, tq, 1), F32),
                        pltpu.VMEM((2, tq, LANES), F32)],
        compiler_params=_cp("parallel", "parallel", "arbitrary"),
        name="attn_" + mode,
    )(q, k, v, *extra)


def _cumsum_rows(tri, x):
    hi = x.astype(BF16)
    r1 = x - hi.astype(F32)
    mid = r1.astype(BF16)
    lo = (r1 - mid.astype(F32)).astype(BF16)
    return _mm(tri, hi) + _mm(tri, mid) + _mm(tri, lo)


def _gla_scan_body(n_chunks, *refs):
    (qf_ref, kf_ref, vf_ref, vtf_ref, af_ref, qb_ref, kb_ref, vb_ref, vtb_ref, ab_ref,
     of_ref, ob_ref, sf_sc, sb_sc) = refs
    C = GLA_CHUNK

    @pl.when(pl.program_id(2) == 0)
    def _():
        sf_sc[...] = jnp.zeros_like(sf_sc)
        sb_sc[...] = jnp.zeros_like(sb_sc)

    r = lax.broadcasted_iota(jnp.int32, (C, C), 0)
    cidx = lax.broadcasted_iota(jnp.int32, (C, C), 1)
    lower = cidx <= r

    def chunk(ci, q_ref, k_ref, v_ref, vt_ref, a_ref, o_ref, s_sc, reverse):
        rows = slice(ci * C, (ci + 1) * C)
        keep = jnp.logical_not(lower) | (cidx == r) if reverse else lower
        tri = jnp.where(keep, 1.0, 0.0).astype(BF16)
        cum = _cumsum_rows(tri, a_ref[0, rows, :])
        total = cum[0:1, :] if reverse else cum[C - 1:C, :]
        q = q_ref[0, rows, :]
        k = k_ref[0, rows, :]
        qd = (q * jnp.exp(cum)).astype(BF16)
        ki = (k * jnp.exp(-cum)).astype(BF16)
        kd = (k * jnp.exp(total - cum)).astype(BF16)
        att = jnp.where(keep, _nt(qd, ki), 0.0).astype(BF16)
        st = s_sc[...]
        o_ref[0, rows, :] = _nt(qd, st.astype(BF16)) + _mm(att, v_ref[0, rows, :])
        s_sc[...] = st * jnp.exp(total) + _mm(vt_ref[0, :, rows], kd)

    for ci in range(n_chunks):
        chunk(ci, qf_ref, kf_ref, vf_ref, vtf_ref, af_ref, of_ref, sf_sc, False)
        chunk(n_chunks - 1 - ci, qb_ref, kb_ref, vb_ref, vtb_ref, ab_ref, ob_ref, sb_sc, True)


def _gla_scan(q, k, v, vt, af, ab, n_lat):
    B, T, dk = q.shape
    dm = v.shape[-1]
    dkh, dvh = dk // GLA_HEADS, dm // GLA_HEADS
    lb = 2 * GLA_CHUNK
    nb = T // lb
    n_lat_b = n_lat // lb
    n_ctx_b = nb - n_lat_b

    def fwd(j):
        return jnp.where(j < n_ctx_b, n_lat_b + j, j - n_ctx_b)

    def bwd(j):
        return jnp.where(j < n_ctx_b, nb - 1 - j, nb - 1 - j)

    def specs(blk):
        return [pl.BlockSpec((1, lb, dkh), lambda b, h, j: (b, blk(j), h)),
                pl.BlockSpec((1, lb, dkh), lambda b, h, j: (b, blk(j), h)),
                pl.BlockSpec((1, lb, dvh), lambda b, h, j: (b, blk(j), h)),
                pl.BlockSpec((1, dvh, lb), lambda b, h, j: (b, h, blk(j))),
                pl.BlockSpec((1, lb, dkh), lambda b, h, j: (b, blk(j), h))]

    o_sds = jax.ShapeDtypeStruct((B, T, dm), F32)
    return pl.pallas_call(
        functools.partial(_gla_scan_body, lb // GLA_CHUNK),
        grid=(B, GLA_HEADS, nb),
        in_specs=specs(fwd) + specs(bwd),
        out_specs=[pl.BlockSpec((1, lb, dvh), lambda b, h, j: (b, fwd(j), h)),
                   pl.BlockSpec((1, lb, dvh), lambda b, h, j: (b, bwd(j), h))],
        out_shape=[o_sds, o_sds],
        scratch_shapes=[pltpu.VMEM((dvh, dkh), F32), pltpu.VMEM((dvh, dkh), F32)],
        compiler_params=_cp("parallel", "parallel", "arbitrary"),
        name="gla_scan",
    )(q, k, v, vt, af, q, k, v, vt, ab)


def _moe_body(h_ref, g_ref, wg_ref, wu_ref, wd_ref, y_ref):
    e = pl.program_id(1)
    h = h_ref[...]
    gates = g_ref[...]
    lane = lax.broadcasted_iota(jnp.int32, gates.shape, 1)
    gate = jnp.sum(jnp.where(lane == e, gates, 0.0), axis=-1, keepdims=True)
    a = _mm(h, wg_ref[0])
    u = _mm(h, wu_ref[0])
    act = (_silu(a) * u * gate).astype(BF16)
    y = _mm(act, wd_ref[0])

    @pl.when(e == 0)
    def _():
        y_ref[...] = y

    @pl.when(e > 0)
    def _():
        y_ref[...] += y


def _moe(h2, gates, wg, wu, wd):
    n, d = h2.shape
    ne, _, f = wg.shape
    tm = math.gcd(n, MOE_TILE)
    return pl.pallas_call(
        _moe_body,
        grid=(n // tm, ne),
        in_specs=[pl.BlockSpec((tm, d), lambda t, e: (t, 0)),
                  pl.BlockSpec((tm, LANES), lambda t, e: (t, 0)),
                  pl.BlockSpec((1, d, f), lambda t, e: (e, 0, 0)),
                  pl.BlockSpec((1, d, f), lambda t, e: (e, 0, 0)),
                  pl.BlockSpec((1, f, d), lambda t, e: (e, 0, 0))],
        out_specs=pl.BlockSpec((tm, d), lambda t, e: (t, 0)),
        out_shape=jax.ShapeDtypeStruct((n, d), F32),
        compiler_params=_cp("parallel", "arbitrary"),
        name="moe",
    )(h2, gates, wg, wu, wd)


def _rope_angles(n_lat, n_ctx, rot_dim):
    rows = n_lat // GRID_W
    row = jnp.repeat(jnp.arange(rows, dtype=F32), GRID_W)
    col = jnp.tile(jnp.arange(GRID_W, dtype=F32), rows)
    n_freq = rot_dim // 4
    inv_freq = ROPE_BASE ** (-jnp.arange(n_freq, dtype=F32) / n_freq)
    ang = jnp.concatenate([row[:, None] * inv_freq, col[:, None] * inv_freq], axis=-1)
    cos = jnp.concatenate([jnp.cos(ang), jnp.ones((n_ctx, rot_dim // 2), F32)], axis=0)
    sin = jnp.concatenate([jnp.sin(ang), jnp.zeros((n_ctx, rot_dim // 2), F32)], axis=0)
    return cos, sin


def _swap_halves(w, width):
    lead = w.shape[:-1]
    w = w.reshape(*lead, -1, 2, width // 2)
    return jnp.flip(w, axis=-2).reshape(*lead, -1)


def kernel(x, c, ctx, c_ctx, w_ada, b_ada, norm_mix, norm_ffn, mla_w_dqkv, mla_q_norm, mla_w_uq, mla_kv_norm, mla_w_ukv, mla_w_o, diff_w_qkv, diff_lambda, diff_subln, diff_w_o, gla_w_in, gla_w_gate_up, gla_b_gate, gla_head_norm, gla_w_o, moe_w_group, moe_b_group, moe_w_router, moe_b_router, moe_w_gate, moe_w_up, moe_w_down, final_norm):
    B, n_lat, D = x.shape
    n_ctx = ctx.shape[1]
    T = n_lat + n_ctx
    depth = w_ada.shape[0]
    assert n_lat % ROW_TILE == 0 and n_ctx % ROW_TILE == 0 and B + 1 <= 8 and D % LANES == 0
    R = _Rows(B, T, n_lat)

    cvec = jnp.zeros((8, D), F32).at[:B].set(c).at[B].set(c_ctx)
    mods = _ada_all(cvec, w_ada, b_ada).reshape(depth, 8, N_MOD, D)

    stream = jnp.concatenate([x, ctx], axis=1)
    y_prev = None

    cos_a, sin_a = _rope_angles(n_lat, n_ctx, MLA_ROPE)
    ones, zeros = jnp.ones((T, MLA_NOPE), F32), jnp.zeros((T, LANES - MLA_NOPE - MLA_ROPE), F32)
    mla_cos = jnp.concatenate([ones, cos_a, cos_a, zeros], axis=1)
    mla_sin = jnp.concatenate([0 * ones, -sin_a, sin_a, zeros], axis=1)
    cos_b, sin_b = _rope_angles(n_lat, n_ctx, DIFF_HEAD_DIM)
    diff_cos = jnp.concatenate([cos_b] * 4, axis=1)
    diff_sin = jnp.concatenate([-sin_b, sin_b] * 2, axis=1)
    diff_scale = DIFF_HEAD_DIM ** -0.5

    for li in range(depth):
        last = li == depth - 1
        kind, j = li % 3, li // 3
        mod = mods[li]
        has_prev = y_prev is not None
        head = [stream] + ([y_prev, mods[li - 1]] if has_prev else []) + [mod, norm_mix[li].reshape(1, D)]
        head_specs = [R.rows(D)] + ([R.rows(D), R.mod(D)] if has_prev else []) + [R.mod(D), R.full(head[-1])]
        x1_out = ([R.out(D, F32)], [R.rows(D)]) if has_prev else ([], [])

        if kind == 0:
            wd = mla_w_dqkv[j]
            pe_w = wd[:, MLA_Q_RANK + MLA_KV_RANK:]
            padl, padr = jnp.zeros((D, MLA_NOPE), F32), jnp.zeros((D, LANES - MLA_NOPE - MLA_ROPE), F32)
            wdown = jnp.concatenate([wd[:, :MLA_Q_RANK + MLA_KV_RANK], padl, pe_w, padr,
                                     padl, _swap_halves(pe_w, MLA_ROPE), padr], axis=1).astype(BF16)
            wq = mla_w_uq[j].reshape(MLA_Q_RANK, MLA_HEADS, MLA_NOPE + MLA_ROPE)
            pad = jnp.zeros((MLA_Q_RANK, MLA_HEADS, LANES - MLA_NOPE - MLA_ROPE), F32)
            wq_a = jnp.concatenate([wq, pad], axis=-1).reshape(MLA_Q_RANK, -1).astype(BF16)
            wq_s = jnp.concatenate([0 * wq[..., :MLA_NOPE], _swap_halves(wq[..., MLA_NOPE:], MLA_ROPE), pad],
                                   axis=-1).reshape(MLA_Q_RANK, -1).astype(BF16)
            wkv = mla_w_ukv[j].reshape(MLA_KV_RANK, MLA_HEADS, MLA_NOPE + MLA_V)
            wk = jnp.concatenate([wkv[..., :MLA_NOPE], jnp.zeros((MLA_KV_RANK, MLA_HEADS, LANES - MLA_NOPE), F32)],
                                 axis=-1).reshape(MLA_KV_RANK, -1).astype(BF16)
            wv = wkv[..., MLA_NOPE:].reshape(MLA_KV_RANK, -1).astype(BF16)
            ins = head + [wdown, mla_q_norm[j].reshape(1, -1), mla_kv_norm[j].reshape(1, -1),
                          wq_a, wq_s, wk, wv, mla_cos, mla_sin]
            specs = head_specs + [R.full(a) for a in ins[len(head):-2]] + [R.table(LANES), R.table(LANES)]
            outs = pl.pallas_call(
                functools.partial(_mla_proj_body, has_prev), grid=R.grid, in_specs=specs,
                out_specs=x1_out[1] + [R.rows(MLA_HEADS * LANES), R.rows(MLA_HEADS * LANES), R.rows(MLA_HEADS * MLA_V)],
                out_shape=x1_out[0] + [R.out(MLA_HEADS * LANES, BF16), R.out(MLA_HEADS * LANES, BF16),
                                       R.out(MLA_HEADS * MLA_V, BF16)],
                compiler_params=_cp("parallel", "parallel"), name="mla_proj")(*ins)
            if has_prev:
                stream = outs[0]
            q, k, v = outs[-3:]
            scale = (MLA_NOPE + MLA_ROPE) ** -0.5
            mix = [_attention("mla", q, k, v, n_lat, n_ctx, scale * math.log2(math.e))]
            mix_specs = [R.rows(D)]
            w_o = mla_w_o[j]
        elif kind == 1:
            lam_init = 0.8 - 0.6 * math.exp(-0.3 * li)
            wq, wk, wv = jnp.split(diff_w_qkv[j], 3, axis=1)
            w = jnp.concatenate([wq, wk, wv, _swap_halves(wq, DIFF_HEAD_DIM), _swap_halves(wk, DIFF_HEAD_DIM)],
                                axis=1).astype(BF16)
            ins = head + [w, diff_cos * diff_scale, diff_sin * diff_scale, diff_cos, diff_sin]
            specs = head_specs + [R.full(w)] + [R.table(LANES)] * 4
            outs = pl.pallas_call(
                functools.partial(_diff_proj_body, has_prev), grid=R.grid, in_specs=specs,
                out_specs=x1_out[1] + [R.rows(D)] * 3, out_shape=x1_out[0] + [R.out(D, BF16)] * 3,
                compiler_params=_cp("parallel", "parallel"), name="diff_proj")(*ins)
            if has_prev:
                stream = outs[0]
            q, k, v = outs[-3:]
            mix = [_attention("diff", q, k, v, n_lat, n_ctx, math.log2(math.e),
                              extra=(diff_lambda[j], diff_subln[j].reshape(1, -1)), lam_init=lam_init)]
            mix_specs = [R.rows(D)]
            w_o = diff_w_o[j]
        else:
            dk = D // 2
            win = gla_w_in[j]
            zpad = jnp.zeros((D, LANES - 2 * GLA_GATE_RANK), F32)
            w = jnp.concatenate([win, zpad], axis=1).astype(BF16)
            wvt = win[:, 2 * dk:2 * dk + D].T.astype(BF16)
            wgu = jnp.zeros((LANES, 2 * dk), F32)
            wgu = wgu.at[:GLA_GATE_RANK, :dk].set(gla_w_gate_up[j, 0])
            wgu = wgu.at[GLA_GATE_RANK:2 * GLA_GATE_RANK, dk:].set(gla_w_gate_up[j, 1]).astype(BF16)
            bg = gla_b_gate[j].reshape(1, 2 * dk)
            ins = head + [w, wvt, wgu, bg]
            specs = head_specs + [R.full(a) for a in ins[len(head):]]
            vt_spec = pl.BlockSpec((1, D, R.tm), lambda b, t: (b, 0, t))
            outs = pl.pallas_call(
                functools.partial(_gla_proj_body, has_prev), grid=R.grid, in_specs=specs,
                out_specs=x1_out[1] + [R.rows(dk), R.rows(dk), R.rows(D), vt_spec, R.rows(D), R.rows(dk), R.rows(dk)],
                out_shape=x1_out[0] + [R.out(dk, F32), R.out(dk, F32), R.out(D, BF16),
                                       jax.ShapeDtypeStruct((B, D, T), BF16), R.out(D, F32),
                                       R.out(dk, F32), R.out(dk, F32)],
                compiler_params=_cp("parallel", "parallel"), name="gla_proj")(*ins)
            if has_prev:
                stream = outs[0]
            q, k, v, vt, g, af, ab = outs[-7:]
            o_f, o_b = _gla_scan(q, k, v, vt, af, ab, n_lat)
            hn = gla_head_norm[j].reshape(1, -1)
            mix = [o_f, o_b, g, hn]
            mix_specs = [R.rows(D), R.rows(D), R.rows(D), R.full(hn)]
            w_o = gla_w_o[j]

        wr = jnp.concatenate([moe_w_group[li], jnp.moveaxis(moe_w_router[li], 0, 1).reshape(D, -1)], axis=1)
        n_r = wr.shape[1]
        wr = jnp.concatenate([wr, jnp.zeros((D, LANES - n_r), F32)], axis=1).astype(BF16)
        br = jnp.concatenate([moe_b_group[li], moe_b_router[li].reshape(-1), jnp.zeros((LANES - n_r,), F32)]).reshape(1, LANES)
        tail = [mod, norm_ffn[li].reshape(1, D), w_o.astype(BF16), wr, br]
        stream, h2, gates = pl.pallas_call(
            functools.partial(_post_body, "gla" if kind == 2 else "attn"), grid=R.grid,
            in_specs=[R.rows(D)] + mix_specs + [R.mod(D)] + [R.full(a) for a in tail[1:]],
            out_specs=[R.rows(D), R.rows(D), R.rows(LANES)],
            out_shape=[R.out(D, F32), R.out(D, BF16), R.out(LANES, F32)],
            compiler_params=_cp("parallel", "parallel"), name="post_mixer")(stream, *mix, *tail)

        ne = MOE_GROUPS * MOE_EXPERTS
        f = moe_w_gate.shape[-1]
        y = _moe(h2.reshape(B * T, D), gates.reshape(B * T, LANES),
                 moe_w_gate[li].reshape(ne, D, f).astype(BF16), moe_w_up[li].reshape(ne, D, f).astype(BF16),
                 moe_w_down[li].reshape(ne, f, D).astype(BF16))
        y_prev = y.reshape(B, T, D)

    RL = _Rows(B, n_lat, n_lat)
    fin = final_norm.reshape(1, D)
    return pl.pallas_call(
        _final_body, grid=RL.grid,
        in_specs=[RL.rows(D), RL.rows(D), RL.mod(D), RL.full(fin)],
        out_specs=RL.rows(D), out_shape=RL.out(D, F32),
        compiler_params=_cp("parallel", "parallel"), name="final_norm")(stream, y_prev, mods[depth - 1], fin)
```

```python
import functools
import math

import jax
import jax.numpy as jnp
from jax import lax
from jax.experimental import pallas as pl
from jax.experimental.pallas import tpu as pltpu

F32 = jnp.float32
BF16 = jnp.bfloat16

EPS = 1e-6
ROPE_BASE = 10000.0
GRID_W = 64
N_MOD = 6
LANES = 128

MLA_HEADS = 16
MLA_Q_RANK = 256
MLA_KV_RANK = 256
MLA_NOPE = 64
MLA_ROPE = 32
MLA_V = 64
DIFF_HEAD_DIM = 64
GLA_HEADS = 4
GLA_GATE_RANK = 16
GLA_GATE_NORM = 16.0
GLA_CHUNK = 128
MOE_GROUPS = 4
MOE_EXPERTS = 4

ROW_TILE = 256
MOE_TILE = 1024
ATTN_KEYS = 1024
ATTN_UNROLL = 8
VMEM_LIMIT = 52 * 1024 * 1024


def _cp(*sem):
    return pltpu.CompilerParams(dimension_semantics=sem, vmem_limit_bytes=VMEM_LIMIT)


def _mm(a, b):
    return jnp.dot(a, b, preferred_element_type=F32)


def _nt(a, b):
    return lax.dot_general(a, b, (((1,), (1,)), ((), ())), preferred_element_type=F32)


def _rms(x):
    return x * lax.rsqrt(jnp.mean(x * x, axis=-1, keepdims=True) + EPS)


def _silu(x):
    return x * (1.0 / (1.0 + jnp.exp(-x)))


def _ada_body(c_ref, w_ref, b_ref, o_ref):
    c = c_ref[...]
    o_ref[0] = _mm(_silu(c).astype(BF16), w_ref[0].astype(BF16)) + b_ref[0]


def _ada_all(cvec, w_ada, b_ada):
    depth, d, n6 = w_ada.shape
    tn = 1536
    return pl.pallas_call(
        _ada_body,
        grid=(depth, n6 // tn),
        in_specs=[pl.BlockSpec((8, d), lambda l, j: (0, 0)),
                  pl.BlockSpec((1, d, tn), lambda l, j: (l, 0, j)),
                  pl.BlockSpec((1, 1, tn), lambda l, j: (l, 0, j))],
        out_specs=pl.BlockSpec((1, 8, tn), lambda l, j: (l, 0, j)),
        out_shape=jax.ShapeDtypeStruct((depth, 8, n6), F32),
        compiler_params=_cp("parallel", "parallel"),
        name="adaln",
    )(cvec, w_ada, b_ada.reshape(depth, 1, n6))


class _Rows:
    def __init__(self, B, T, n_lat, tm=ROW_TILE):
        self.B, self.T, self.tm = B, T, tm
        self.n_lat_tiles = n_lat // tm
        self.grid = (B, T // tm)

    def rows(self, c):
        return pl.BlockSpec((1, self.tm, c), lambda b, t: (b, t, 0))

    def mod(self, d):
        nl, B = self.n_lat_tiles, self.B
        return pl.BlockSpec((1, N_MOD, d), lambda b, t: (jnp.where(t < nl, b, B), 0, 0))

    def cols(self, c):
        return pl.BlockSpec((1, c, self.tm), lambda b, t: (b, 0, t))

    def table(self, c):
        return pl.BlockSpec((self.tm, c), lambda b, t: (t, 0))

    @staticmethod
    def full(a):
        nd = a.ndim
        return pl.BlockSpec(a.shape, lambda b, t: (0,) * nd)

    def out(self, c, dtype):
        return jax.ShapeDtypeStruct((self.B, self.T, c), dtype)


def _prologue(refs, has_prev, k_shift):
    if has_prev:
        x_ref, y_ref, modp_ref, mod_ref, gain_ref = refs[:5]
        x = x_ref[0] + modp_ref[0, 5:6, :] * y_ref[0]
        rest = refs[5:]
    else:
        x_ref, mod_ref, gain_ref = refs[:3]
        x = x_ref[0]
        rest = refs[3:]
    h = _rms(x) * gain_ref[...]
    h = h * (1.0 + mod_ref[0, k_shift + 1:k_shift + 2, :]) + mod_ref[0, k_shift:k_shift + 1, :]
    return x, h, rest


def _mla_proj_body(has_prev, *refs):
    x, h, rest = _prologue(refs, has_prev, 0)
    (wd_ref, qn_ref, kvn_ref, wq_ref, wqs_ref, wk_ref, wvt_ref, cq_ref, sq_ref, c_ref, s_ref), outs = rest[:11], rest[11:]
    if has_prev:
        x1_ref, q_ref, k_ref, vt_ref = outs
        x1_ref[0] = x
    else:
        q_ref, k_ref, vt_ref = outs
    d = _mm(h.astype(BF16), wd_ref[...])
    cos, sin = c_ref[...], s_ref[...]
    cos_q, sin_q = cq_ref[...], sq_ref[...]
    cq = (_rms(d[:, :MLA_Q_RANK]) * qn_ref[...]).astype(BF16)
    ckv = (_rms(d[:, MLA_Q_RANK:MLA_Q_RANK + MLA_KV_RANK]) * kvn_ref[...]).astype(BF16)
    o = MLA_Q_RANK + MLA_KV_RANK
    pe = d[:, o:o + LANES] * cos + d[:, o + LANES:o + 2 * LANES] * sin
    qa = _mm(cq, wq_ref[...])
    qb = _mm(cq, wqs_ref[...])
    kn = _mm(ckv, wk_ref[...])
    for hd in range(MLA_HEADS):
        sl = slice(hd * LANES, (hd + 1) * LANES)
        q_ref[0, :, sl] = (qa[:, sl] * cos_q + qb[:, sl] * sin_q).astype(BF16)
        k_ref[0, :, sl] = (kn[:, sl] + pe).astype(BF16)
    vt_ref[0] = _nt(wvt_ref[...], ckv).astype(BF16)


def _diff_proj_body(has_prev, *refs):
    x, h, rest = _prologue(refs, has_prev, 0)
    (w_ref, wvt_ref, cq_ref, sq_ref, ck_ref, sk_ref), outs = rest[:6], rest[6:]
    if has_prev:
        x1_ref, q_ref, k_ref, vt_ref = outs
        x1_ref[0] = x
    else:
        q_ref, k_ref, vt_ref = outs
    dm = h.shape[-1]
    hb = h.astype(BF16)
    p = _mm(hb, w_ref[...])
    cq, sq, ck, sk = cq_ref[...], sq_ref[...], ck_ref[...], sk_ref[...]
    for hd in range(dm // LANES):
        sl = slice(hd * LANES, (hd + 1) * LANES)
        q_ref[0, :, sl] = (p[:, sl] * cq + p[:, 2 * dm + hd * LANES:2 * dm + (hd + 1) * LANES] * sq).astype(BF16)
        k_ref[0, :, sl] = (p[:, dm + hd * LANES:dm + (hd + 1) * LANES] * ck
                           + p[:, 3 * dm + hd * LANES:3 * dm + (hd + 1) * LANES] * sk).astype(BF16)
    vt_ref[0] = _nt(wvt_ref[...], hb).astype(BF16)


def _gla_proj_body(has_prev, *refs):
    x, h, rest = _prologue(refs, has_prev, 0)
    (w_ref, wvt_ref, wgu_ref, bg_ref), outs = rest[:4], rest[4:]
    if has_prev:
        x1_ref, q_ref, k_ref, v_ref, vt_ref, g_ref, af_ref, ab_ref = outs
        x1_ref[0] = x
    else:
        q_ref, k_ref, v_ref, vt_ref, g_ref, af_ref, ab_ref = outs
    dm = h.shape[-1]
    dk = dm // 2
    hb = h.astype(BF16)
    p = _mm(hb, w_ref[...])
    q_ref[0] = p[:, :dk] * ((dk // GLA_HEADS) ** -0.5)
    k_ref[0] = p[:, dk:2 * dk]
    v_ref[0] = p[:, 2 * dk:2 * dk + dm].astype(BF16)
    vt_ref[0] = _nt(wvt_ref[...], hb).astype(BF16)
    g_ref[0] = p[:, 2 * dk + dm:2 * dk + 2 * dm]
    z = p[:, 2 * dk + 2 * dm:].astype(BF16)
    a = _mm(z, wgu_ref[...]) + bg_ref[...]
    ls = (jnp.minimum(a, 0.0) - jnp.log(1.0 + jnp.exp(-jnp.abs(a)))) / GLA_GATE_NORM
    af_ref[0] = ls[:, :dk]
    ab_ref[0] = ls[:, dk:]


def _route(logits):
    lane = lax.broadcasted_iota(jnp.int32, logits.shape, 1)

    def col(i):
        return jnp.sum(jnp.where(lane == i, logits, 0.0), axis=-1, keepdims=True)

    lg = [col(i) for i in range(MOE_GROUPS)]
    gmax = functools.reduce(jnp.maximum, lg)
    onehot, taken = [], None
    for i in range(MOE_GROUPS):
        hit = lg[i] == gmax
        if taken is not None:
            hit = jnp.logical_and(hit, jnp.logical_not(taken))
        taken = hit if taken is None else jnp.logical_or(taken, hit)
        onehot.append(hit)
    grp_w = 1.0 / functools.reduce(lambda a, b: a + b, [jnp.exp(l - gmax) for l in lg])
    le = []
    for e in range(MOE_EXPERTS):
        v = None
        for g in range(MOE_GROUPS):
            c = jnp.where(onehot[g], col(MOE_GROUPS + g * MOE_EXPERTS + e), 0.0)
            v = c if v is None else v + c
        le.append(v)
    vmax = functools.reduce(jnp.maximum, le)
    num = []
    for e in range(MOE_EXPERTS):
        rank = None
        for o in range(MOE_EXPERTS):
            if o == e:
                continue
            ahead = (le[o] >= le[e]) if o < e else (le[o] > le[e])
            r = jnp.where(ahead, 1.0, 0.0)
            rank = r if rank is None else rank + r
        num.append(jnp.where(rank < 2.0, jnp.exp(le[e] - vmax), 0.0))
    scale = grp_w / functools.reduce(lambda a, b: a + b, num)
    gates = jnp.zeros(logits.shape, F32)
    for g in range(MOE_GROUPS):
        for e in range(MOE_EXPERTS):
            val = jnp.where(onehot[g], num[e] * scale, 0.0)
            gates = jnp.where(lane == g * MOE_EXPERTS + e, val, gates)
    return gates


def _post_body(kind, *refs):
    x1_ref = refs[0]
    if kind == "gla":
        of_ref, ob_ref, g_ref, hn_ref = refs[1:5]
        rest = refs[5:]
        o = of_ref[0] + ob_ref[0]
        hn = hn_ref[...]
        dv = hn.shape[-1]
        parts = [_rms(o[:, i * dv:(i + 1) * dv]) * hn for i in range(o.shape[-1] // dv)]
        a = (jnp.concatenate(parts, axis=-1) * _silu(g_ref[0])).astype(BF16)
    else:
        a = refs[1][0]
        rest = refs[2:]
    mod_ref, gain_ref, wo_ref, wr_ref, br_ref, x2_ref, h2_ref, gate_ref = rest
    x2 = x1_ref[0] + mod_ref[0, 2:3, :] * _mm(a, wo_ref[...])
    x2_ref[0] = x2
    h2 = _rms(x2) * gain_ref[...]
    h2 = (h2 * (1.0 + mod_ref[0, 4:5, :]) + mod_ref[0, 3:4, :]).astype(BF16)
    h2_ref[0] = h2
    gate_ref[0] = _route(_mm(h2, wr_ref[...]) + br_ref[...])


def _final_body(x_ref, y_ref, mod_ref, gain_ref, o_ref):
    x = x_ref[0] + mod_ref[0, 5:6, :] * y_ref[0]
    o_ref[0] = _rms(x) * gain_ref[...]


def _attn_body(mode, n_lat, n_ctx, tk, lam_init, *refs):
    if mode == "mla":
        q_ref, k_ref, vt_ref, o_ref, s_sc, m_sc, l_sc, acc_sc = refs
    else:
        q_ref, k_ref, vt_ref, lam_ref, sub_ref, o_ref, s_sc, m_sc, l_sc, acc_sc = refs
    tq = q_ref.shape[1]
    dv = acc_sc.shape[1]
    n_steps = n_lat // tk
    is_lat = pl.program_id(2) < (n_lat // tq)

    def q_of(i):
        if mode == "mla":
            return q_ref[0, :, i * LANES:(i + 1) * LANES]
        q = q_ref[0]
        lane = lax.broadcasted_iota(jnp.int32, q.shape, 1)
        keep = (lane < DIFF_HEAD_DIM) if i == 0 else (lane >= DIFF_HEAD_DIM)
        return jnp.where(keep, q, jnp.zeros_like(q))

    qs = [q_of(0), q_of(1)]

    def scores(i, rows):
        k = k_ref[0, rows, i * LANES:(i + 1) * LANES] if mode == "mla" else k_ref[0, rows, :]
        return _nt(k, qs[i])

    def vt_of(i, cols):
        return vt_ref[0, i * dv:(i + 1) * dv, cols] if mode == "mla" else vt_ref[0, :, cols]

    def absorb(i, s, cols, first=False):
        m_new = jnp.max(s, axis=0, keepdims=True)
        if not first:
            m_old = m_sc[i]
            m_new = jnp.maximum(m_old, m_new)
            alpha = jnp.exp2(m_old - m_new)
        p = jnp.exp2(s - m_new)
        l = jnp.sum(p, axis=0, keepdims=True)
        pv = _mm(vt_of(i, cols), p.astype(BF16))
        l_sc[i] = l if first else alpha * l_sc[i] + l
        acc_sc[i] = pv if first else alpha * acc_sc[i] + pv
        m_sc[i] = m_new

    ctx = slice(n_lat, n_lat + n_ctx)
    for i in range(2):
        absorb(i, scores(i, ctx), ctx, first=True)

    @pl.when(is_lat)
    def _():
        def stage(cur, slot, nxt):
            for i in range(2):
                if nxt is not None:
                    s_next = scores(i, pl.ds(pl.multiple_of(nxt, tk), tk))
                absorb(i, s_sc[slot, i], pl.ds(pl.multiple_of(cur, tk), tk))
                if nxt is not None:
                    s_sc[1 - slot, i] = s_next

        for i in range(2):
            s_sc[0, i] = scores(i, slice(0, tk))
        if n_steps % 2 == 0 and n_steps > ATTN_UNROLL:
            def step(j, carry):
                base = 2 * j * tk
                stage(base, 0, base + tk)
                stage(base + tk, 1, base + 2 * tk)
                return carry

            lax.fori_loop(0, n_steps // 2 - 1, step, 0)
            stage((n_steps - 2) * tk, 0, (n_steps - 1) * tk)
            stage((n_steps - 1) * tk, 1, None)
        else:
            for cidx in range(n_steps):
                stage(cidx * tk, cidx % 2, (cidx + 1) * tk if cidx + 1 < n_steps else None)

    o0 = acc_sc[0] * (1.0 / l_sc[0])
    o1 = acc_sc[1] * (1.0 / l_sc[1])
    if mode == "mla":
        o = jnp.concatenate([o0, o1], axis=0)
    else:
        lam = lam_ref[...]
        lam_full = (jnp.exp(jnp.sum(lam[0:1] * lam[1:2], axis=-1, keepdims=True))
                    - jnp.exp(jnp.sum(lam[2:3] * lam[3:4], axis=-1, keepdims=True)) + lam_init)
        o = o0 - lam_full * o1
        o = o * lax.rsqrt(jnp.mean(o * o, axis=0, keepdims=True) + EPS) * sub_ref[...] * (1.0 - lam_init)
    o_ref[0] = o.T.astype(BF16)


def _attention(mode, q, k, vt, n_lat, n_ctx, extra=(), lam_init=0.0):
    B, T, _ = q.shape
    tq = ROW_TILE
    tk = math.gcd(n_lat, ATTN_KEYS)
    qw = 2 * LANES if mode == "mla" else LANES
    dv = MLA_V if mode == "mla" else LANES
    nh = q.shape[-1] // qw
    in_specs = [pl.BlockSpec((1, tq, qw), lambda b, h, i: (b, i, h)),
                pl.BlockSpec((1, T, qw), lambda b, h, i: (b, 0, h)),
                pl.BlockSpec((1, LANES, T), lambda b, h, i: (b, h, 0))]
    for a in extra:
        in_specs.append(pl.BlockSpec(a.shape, lambda b, h, i, nd=a.ndim: (0,) * nd))
    return pl.pallas_call(
        functools.partial(_attn_body, mode, n_lat, n_ctx, tk, lam_init),
        grid=(B, nh, T // tq),
        in_specs=in_specs,
        out_specs=pl.BlockSpec((1, tq, LANES), lambda b, h, i: (b, i, h)),
        out_shape=jax.ShapeDtypeStruct((B, T, nh * LANES), BF16),
        scratch_shapes=[pltpu.VMEM((2, 2, tk, tq), F32), pltpu.VMEM((2, 1, tq), F32),
                        pltpu.VMEM((2, 1, tq), F32), pltpu.VMEM((2, dv, tq), F32)],
        compiler_params=_cp("parallel", "parallel", "arbitrary"),
        name="attn_" + mode,
    )(q, k, vt, *extra)


def _cumsum_rows(tri, x):
    hi = x.astype(BF16)
    r1 = x - hi.astype(F32)
    mid = r1.astype(BF16)
    lo = (r1 - mid.astype(F32)).astype(BF16)
    return _mm(tri, hi) + _mm(tri, mid) + _mm(tri, lo)


def _gla_scan_body(n_chunks, *refs):
    (qf_ref, kf_ref, vf_ref, vtf_ref, af_ref, qb_ref, kb_ref, vb_ref, vtb_ref, ab_ref,
     of_ref, ob_ref, sf_sc, sb_sc) = refs
    C = GLA_CHUNK

    @pl.when(pl.program_id(2) == 0)
    def _():
        sf_sc[...] = jnp.zeros_like(sf_sc)
        sb_sc[...] = jnp.zeros_like(sb_sc)

    r = lax.broadcasted_iota(jnp.int32, (C, C), 0)
    cidx = lax.broadcasted_iota(jnp.int32, (C, C), 1)
    lower = cidx <= r

    def chunk(ci, q_ref, k_ref, v_ref, vt_ref, a_ref, o_ref, s_sc, reverse):
        rows = slice(ci * C, (ci + 1) * C)
        keep = jnp.logical_not(lower) | (cidx == r) if reverse else lower
        tri = jnp.where(keep, 1.0, 0.0).astype(BF16)
        cum = _cumsum_rows(tri, a_ref[0, rows, :])
        total = cum[0:1, :] if reverse else cum[C - 1:C, :]
        q = q_ref[0, rows, :]
        k = k_ref[0, rows, :]
        qd = (q * jnp.exp(cum)).astype(BF16)
        ki = (k * jnp.exp(-cum)).astype(BF16)
        kd = (k * jnp.exp(total - cum)).astype(BF16)
        att = jnp.where(keep, _nt(qd, ki), 0.0).astype(BF16)
        st = s_sc[...]
        o_ref[0, rows, :] = _nt(qd, st.astype(BF16)) + _mm(att, v_ref[0, rows, :])
        s_sc[...] = st * jnp.exp(total) + _mm(vt_ref[0, :, rows], kd)

    for ci in range(n_chunks):
        chunk(ci, qf_ref, kf_ref, vf_ref, vtf_ref, af_ref, of_ref, sf_sc, False)
        chunk(n_chunks - 1 - ci, qb_ref, kb_ref, vb_ref, vtb_ref, ab_ref, ob_ref, sb_sc, True)


def _gla_scan(q, k, v, vt, af, ab, n_lat):
    B, T, dk = q.shape
    dm = v.shape[-1]
    dkh, dvh = dk // GLA_HEADS, dm // GLA_HEADS
    lb = 2 * GLA_CHUNK
    nb = T // lb
    n_lat_b = n_lat // lb
    n_ctx_b = nb - n_lat_b

    def fwd(j):
        return jnp.where(j < n_ctx_b, n_lat_b + j, j - n_ctx_b)

    def bwd(j):
        return jnp.where(j < n_ctx_b, nb - 1 - j, nb - 1 - j)

    def specs(blk):
        return [pl.BlockSpec((1, lb, dkh), lambda b, h, j: (b, blk(j), h)),
                pl.BlockSpec((1, lb, dkh), lambda b, h, j: (b, blk(j), h)),
                pl.BlockSpec((1, lb, dvh), lambda b, h, j: (b, blk(j), h)),
                pl.BlockSpec((1, dvh, lb), lambda b, h, j: (b, h, blk(j))),
                pl.BlockSpec((1, lb, dkh), lambda b, h, j: (b, blk(j), h))]

    o_sds = jax.ShapeDtypeStruct((B, T, dm), F32)
    return pl.pallas_call(
        functools.partial(_gla_scan_body, lb // GLA_CHUNK),
        grid=(B, GLA_HEADS, nb),
        in_specs=specs(fwd) + specs(bwd),
        out_specs=[pl.BlockSpec((1, lb, dvh), lambda b, h, j: (b, fwd(j), h)),
                   pl.BlockSpec((1, lb, dvh), lambda b, h, j: (b, bwd(j), h))],
        out_shape=[o_sds, o_sds],
        scratch_shapes=[pltpu.VMEM((dvh, dkh), F32), pltpu.VMEM((dvh, dkh), F32)],
        compiler_params=_cp("parallel", "parallel", "arbitrary"),
        name="gla_scan",
    )(q, k, v, vt, af, q, k, v, vt, ab)


def _moe_body(h_ref, g_ref, wg_ref, wu_ref, wd_ref, y_ref):
    e = pl.program_id(1)
    h = h_ref[...]
    gates = g_ref[...]
    lane = lax.broadcasted_iota(jnp.int32, gates.shape, 1)
    gate = jnp.sum(jnp.where(lane == e, gates, 0.0), axis=-1, keepdims=True)
    a = _mm(h, wg_ref[0])
    u = _mm(h, wu_ref[0])
    act = (_silu(a) * u * gate).astype(BF16)
    y = _mm(act, wd_ref[0])

    @pl.when(e == 0)
    def _():
        y_ref[...] = y

    @pl.when(e > 0)
    def _():
        y_ref[...] += y


def _moe(h2, gates, wg, wu, wd):
    n, d = h2.shape
    ne, _, f = wg.shape
    tm = math.gcd(n, MOE_TILE)
    return pl.pallas_call(
        _moe_body,
        grid=(n // tm, ne),
        in_specs=[pl.BlockSpec((tm, d), lambda t, e: (t, 0)),
                  pl.BlockSpec((tm, LANES), lambda t, e: (t, 0)),
                  pl.BlockSpec((1, d, f), lambda t, e: (e, 0, 0)),
                  pl.BlockSpec((1, d, f), lambda t, e: (e, 0, 0)),
                  pl.BlockSpec((1, f, d), lambda t, e: (e, 0, 0))],
        out_specs=pl.BlockSpec((tm, d), lambda t, e: (t, 0)),
        out_shape=jax.ShapeDtypeStruct((n, d), F32),
        compiler_params=_cp("parallel", "arbitrary"),
        name="moe",
    )(h2, gates, wg, wu, wd)


def _rope_angles(n_lat, n_ctx, rot_dim):
    rows = n_lat // GRID_W
    row = jnp.repeat(jnp.arange(rows, dtype=F32), GRID_W)
    col = jnp.tile(jnp.arange(GRID_W, dtype=F32), rows)
    n_freq = rot_dim // 4
    inv_freq = ROPE_BASE ** (-jnp.arange(n_freq, dtype=F32) / n_freq)
    ang = jnp.concatenate([row[:, None] * inv_freq, col[:, None] * inv_freq], axis=-1)
    cos = jnp.concatenate([jnp.cos(ang), jnp.ones((n_ctx, rot_dim // 2), F32)], axis=0)
    sin = jnp.concatenate([jnp.sin(ang), jnp.zeros((n_ctx, rot_dim // 2), F32)], axis=0)
    return cos, sin


def _swap_halves(w, width):
    lead = w.shape[:-1]
    w = w.reshape(*lead, -1, 2, width // 2)
    return jnp.flip(w, axis=-2).reshape(*lead, -1)


def kernel(x, c, ctx, c_ctx, w_ada, b_ada, norm_mix, norm_ffn, mla_w_dqkv, mla_q_norm, mla_w_uq, mla_kv_norm, mla_w_ukv, mla_w_o, diff_w_qkv, diff_lambda, diff_subln, diff_w_o, gla_w_in, gla_w_gate_up, gla_b_gate, gla_head_norm, gla_w_o, moe_w_group, moe_b_group, moe_w_router, moe_b_router, moe_w_gate, moe_w_up, moe_w_down, final_norm):
    B, n_lat, D = x.shape
    n_ctx = ctx.shape[1]
    T = n_lat + n_ctx
    depth = w_ada.shape[0]
    assert n_lat % ROW_TILE == 0 and n_ctx % ROW_TILE == 0 and B + 1 <= 8 and D % LANES == 0
    R = _Rows(B, T, n_lat)

    cvec = jnp.zeros((8, D), F32).at[:B].set(c).at[B].set(c_ctx)
    mods = _ada_all(cvec, w_ada, b_ada).reshape(depth, 8, N_MOD, D)

    stream = jnp.concatenate([x, ctx], axis=1)
    y_prev = None

    cos_a, sin_a = _rope_angles(n_lat, n_ctx, MLA_ROPE)
    ones, zeros = jnp.ones((T, MLA_NOPE), F32), jnp.zeros((T, LANES - MLA_NOPE - MLA_ROPE), F32)
    mla_cos = jnp.concatenate([ones, cos_a, cos_a, zeros], axis=1)
    mla_sin = jnp.concatenate([0 * ones, -sin_a, sin_a, zeros], axis=1)
    cos_b, sin_b = _rope_angles(n_lat, n_ctx, DIFF_HEAD_DIM)
    diff_cos = jnp.concatenate([cos_b] * 4, axis=1)
    diff_sin = jnp.concatenate([-sin_b, sin_b] * 2, axis=1)
    diff_scale = DIFF_HEAD_DIM ** -0.5 * math.log2(math.e)
    mla_qscale = (MLA_NOPE + MLA_ROPE) ** -0.5 * math.log2(math.e)

    for li in range(depth):
        last = li == depth - 1
        kind, j = li % 3, li // 3
        mod = mods[li]
        has_prev = y_prev is not None
        head = [stream] + ([y_prev, mods[li - 1]] if has_prev else []) + [mod, norm_mix[li].reshape(1, D)]
        head_specs = [R.rows(D)] + ([R.rows(D), R.mod(D)] if has_prev else []) + [R.mod(D), R.full(head[-1])]
        x1_out = ([R.out(D, F32)], [R.rows(D)]) if has_prev else ([], [])

        if kind == 0:
            wd = mla_w_dqkv[j]
            pe_w = wd[:, MLA_Q_RANK + MLA_KV_RANK:]
            padl, padr = jnp.zeros((D, MLA_NOPE), F32), jnp.zeros((D, LANES - MLA_NOPE - MLA_ROPE), F32)
            wdown = jnp.concatenate([wd[:, :MLA_Q_RANK + MLA_KV_RANK], padl, pe_w, padr,
                                     padl, _swap_halves(pe_w, MLA_ROPE), padr], axis=1).astype(BF16)
            wq = mla_w_uq[j].reshape(MLA_Q_RANK, MLA_HEADS, MLA_NOPE + MLA_ROPE)
            pad = jnp.zeros((MLA_Q_RANK, MLA_HEADS, LANES - MLA_NOPE - MLA_ROPE), F32)
            wq_a = jnp.concatenate([wq, pad], axis=-1).reshape(MLA_Q_RANK, -1).astype(BF16)
            wq_s = jnp.concatenate([0 * wq[..., :MLA_NOPE], _swap_halves(wq[..., MLA_NOPE:], MLA_ROPE), pad],
                                   axis=-1).reshape(MLA_Q_RANK, -1).astype(BF16)
            wkv = mla_w_ukv[j].reshape(MLA_KV_RANK, MLA_HEADS, MLA_NOPE + MLA_V)
            wk = jnp.concatenate([wkv[..., :MLA_NOPE], jnp.zeros((MLA_KV_RANK, MLA_HEADS, LANES - MLA_NOPE), F32)],
                                 axis=-1).reshape(MLA_KV_RANK, -1).astype(BF16)
            wvt = wkv[..., MLA_NOPE:].reshape(MLA_KV_RANK, -1).T.astype(BF16)
            ins = head + [wdown, mla_q_norm[j].reshape(1, -1), mla_kv_norm[j].reshape(1, -1),
                          wq_a, wq_s, wk, wvt, mla_cos * mla_qscale, mla_sin * mla_qscale, mla_cos, mla_sin]
            specs = head_specs + [R.full(a) for a in ins[len(head):-4]] + [R.table(LANES)] * 4
            outs = pl.pallas_call(
                functools.partial(_mla_proj_body, has_prev), grid=R.grid, in_specs=specs,
                out_specs=x1_out[1] + [R.rows(MLA_HEADS * LANES), R.rows(MLA_HEADS * LANES), R.cols(MLA_HEADS * MLA_V)],
                out_shape=x1_out[0] + [R.out(MLA_HEADS * LANES, BF16), R.out(MLA_HEADS * LANES, BF16),
                                       jax.ShapeDtypeStruct((B, MLA_HEADS * MLA_V, T), BF16)],
                compiler_params=_cp("parallel", "parallel"), name="mla_proj")(*ins)
            if has_prev:
                stream = outs[0]
            q, k, vt = outs[-3:]
            mix = [_attention("mla", q, k, vt, n_lat, n_ctx)]
            mix_specs = [R.rows(D)]
            w_o = mla_w_o[j]
        elif kind == 1:
            lam_init = 0.8 - 0.6 * math.exp(-0.3 * li)
            wq, wk, wv = jnp.split(diff_w_qkv[j], 3, axis=1)
            w = jnp.concatenate([wq, wk, _swap_halves(wq, DIFF_HEAD_DIM), _swap_halves(wk, DIFF_HEAD_DIM)],
                                axis=1).astype(BF16)
            wvt = wv.T.astype(BF16)
            ins = head + [w, wvt, diff_cos * diff_scale, diff_sin * diff_scale, diff_cos, diff_sin]
            specs = head_specs + [R.full(w), R.full(wvt)] + [R.table(LANES)] * 4
            outs = pl.pallas_call(
                functools.partial(_diff_proj_body, has_prev), grid=R.grid, in_specs=specs,
                out_specs=x1_out[1] + [R.rows(D), R.rows(D), R.cols(D)],
                out_shape=x1_out[0] + [R.out(D, BF16), R.out(D, BF16), jax.ShapeDtypeStruct((B, D, T), BF16)],
                compiler_params=_cp("parallel", "parallel"), name="diff_proj")(*ins)
            if has_prev:
                stream = outs[0]
            q, k, vt = outs[-3:]
            mix = [_attention("diff", q, k, vt, n_lat, n_ctx,
                              extra=(diff_lambda[j], diff_subln[j].reshape(-1, 1)), lam_init=lam_init)]
            mix_specs = [R.rows(D)]
            w_o = diff_w_o[j]
        else:
            dk = D // 2
            win = gla_w_in[j]
            zpad = jnp.zeros((D, LANES - 2 * GLA_GATE_RANK), F32)
            w = jnp.concatenate([win, zpad], axis=1).astype(BF16)
            wvt = win[:, 2 * dk:2 * dk + D].T.astype(BF16)
            wgu = jnp.zeros((LANES, 2 * dk), F32)
            wgu = wgu.at[:GLA_GATE_RANK, :dk].set(gla_w_gate_up[j, 0])
            wgu = wgu.at[GLA_GATE_RANK:2 * GLA_GATE_RANK, dk:].set(gla_w_gate_up[j, 1]).astype(BF16)
            bg = gla_b_gate[j].reshape(1, 2 * dk)
            ins = head + [w, wvt, wgu, bg]
            specs = head_specs + [R.full(a) for a in ins[len(head):]]
            vt_spec = pl.BlockSpec((1, D, R.tm), lambda b, t: (b, 0, t))
            outs = pl.pallas_call(
                functools.partial(_gla_proj_body, has_prev), grid=R.grid, in_specs=specs,
                out_specs=x1_out[1] + [R.rows(dk), R.rows(dk), R.rows(D), vt_spec, R.rows(D), R.rows(dk), R.rows(dk)],
                out_shape=x1_out[0] + [R.out(dk, F32), R.out(dk, F32), R.out(D, BF16),
                                       jax.ShapeDtypeStruct((B, D, T), BF16), R.out(D, F32),
                                       R.out(dk, F32), R.out(dk, F32)],
                compiler_params=_cp("parallel", "parallel"), name="gla_proj")(*ins)
            if has_prev:
                stream = outs[0]
            q, k, v, vt, g, af, ab = outs[-7:]
            o_f, o_b = _gla_scan(q, k, v, vt, af, ab, n_lat)
            hn = gla_head_norm[j].reshape(1, -1)
            mix = [o_f, o_b, g, hn]
            mix_specs = [R.rows(D), R.rows(D), R.rows(D), R.full(hn)]
            w_o = gla_w_o[j]

        wr = jnp.concatenate([moe_w_group[li], jnp.moveaxis(moe_w_router[li], 0, 1).reshape(D, -1)], axis=1)
        n_r = wr.shape[1]
        wr = jnp.concatenate([wr, jnp.zeros((D, LANES - n_r), F32)], axis=1).astype(BF16)
        br = jnp.concatenate([moe_b_group[li], moe_b_router[li].reshape(-1), jnp.zeros((LANES - n_r,), F32)]).reshape(1, LANES)
        tail = [mod, norm_ffn[li].reshape(1, D), w_o.astype(BF16), wr, br]
        stream, h2, gates = pl.pallas_call(
            functools.partial(_post_body, "gla" if kind == 2 else "attn"), grid=R.grid,
            in_specs=[R.rows(D)] + mix_specs + [R.mod(D)] + [R.full(a) for a in tail[1:]],
            out_specs=[R.rows(D), R.rows(D), R.rows(LANES)],
            out_shape=[R.out(D, F32), R.out(D, BF16), R.out(LANES, F32)],
            compiler_params=_cp("parallel", "parallel"), name="post_mixer")(stream, *mix, *tail)

        ne = MOE_GROUPS * MOE_EXPERTS
        f = moe_w_gate.shape[-1]
        y = _moe(h2.reshape(B * T, D), gates.reshape(B * T, LANES),
                 moe_w_gate[li].reshape(ne, D, f).astype(BF16), moe_w_up[li].reshape(ne, D, f).astype(BF16),
                 moe_w_down[li].reshape(ne, f, D).astype(BF16))
        y_prev = y.reshape(B, T, D)

    RL = _Rows(B, n_lat, n_lat)
    fin = final_norm.reshape(1, D)
    return pl.pallas_call(
        _final_body, grid=RL.grid,
        in_specs=[RL.rows(D), RL.rows(D), RL.mod(D), RL.full(fin)],
        out_specs=RL.rows(D), out_shape=RL.out(D, F32),
        compiler_params=_cp("parallel", "parallel"), name="final_norm")(stream, y_prev, mods[depth - 1], fin)
```

```python
import functools
import math

import jax
import jax.numpy as jnp
from jax import lax
from jax.experimental import pallas as pl
from jax.experimental.pallas import tpu as pltpu

F32 = jnp.float32
BF16 = jnp.bfloat16

EPS = 1e-6
ROPE_BASE = 10000.0
GRID_W = 64
N_MOD = 6
LANES = 128

MLA_HEADS = 16
MLA_Q_RANK = 256
MLA_KV_RANK = 256
MLA_NOPE = 64
MLA_ROPE = 32
MLA_V = 64
DIFF_HEAD_DIM = 64
GLA_HEADS = 4
GLA_GATE_RANK = 16
GLA_GATE_NORM = 16.0
GLA_CHUNK = 128
MOE_GROUPS = 4
MOE_EXPERTS = 4

ROW_TILE = 256
MOE_TILE = 1024
MOE_SLOTS = 256
MOE_PAIR = 2
ATTN_KEYS = 1024
ATTN_UNROLL = 8
VMEM_LIMIT = 52 * 1024 * 1024


def _cp(*sem):
    return pltpu.CompilerParams(dimension_semantics=sem, vmem_limit_bytes=VMEM_LIMIT)


def _mm(a, b):
    return jnp.dot(a, b, preferred_element_type=F32)


def _nt(a, b):
    return lax.dot_general(a, b, (((1,), (1,)), ((), ())), preferred_element_type=F32)


def _rms(x):
    return x * lax.rsqrt(jnp.mean(x * x, axis=-1, keepdims=True) + EPS)


def _silu(x):
    return x * (1.0 / (1.0 + jnp.exp(-x)))


def _ada_body(c_ref, w_ref, b_ref, o_ref):
    c = c_ref[...]
    o_ref[0] = _mm(_silu(c).astype(BF16), w_ref[0].astype(BF16)) + b_ref[0]


def _ada_all(cvec, w_ada, b_ada):
    depth, d, n6 = w_ada.shape
    tn = 1536
    return pl.pallas_call(
        _ada_body,
        grid=(depth, n6 // tn),
        in_specs=[pl.BlockSpec((8, d), lambda l, j: (0, 0)),
                  pl.BlockSpec((1, d, tn), lambda l, j: (l, 0, j)),
                  pl.BlockSpec((1, 1, tn), lambda l, j: (l, 0, j))],
        out_specs=pl.BlockSpec((1, 8, tn), lambda l, j: (l, 0, j)),
        out_shape=jax.ShapeDtypeStruct((depth, 8, n6), F32),
        compiler_params=_cp("parallel", "parallel"),
        name="adaln",
    )(cvec, w_ada, b_ada.reshape(depth, 1, n6))


class _Rows:
    def __init__(self, B, T, n_lat, tm=ROW_TILE):
        self.B, self.T, self.tm = B, T, tm
        self.n_lat_tiles = n_lat // tm
        self.grid = (B, T // tm)

    def rows(self, c):
        return pl.BlockSpec((1, self.tm, c), lambda b, t: (b, t, 0))

    def mod(self, d):
        nl, B = self.n_lat_tiles, self.B
        return pl.BlockSpec((1, N_MOD, d), lambda b, t: (jnp.where(t < nl, b, B), 0, 0))

    def cols(self, c):
        return pl.BlockSpec((1, c, self.tm), lambda b, t: (b, 0, t))

    def table(self, c):
        return pl.BlockSpec((self.tm, c), lambda b, t: (t, 0))

    @staticmethod
    def full(a):
        nd = a.ndim
        return pl.BlockSpec(a.shape, lambda b, t: (0,) * nd)

    def out(self, c, dtype):
        return jax.ShapeDtypeStruct((self.B, self.T, c), dtype)


def _prologue(refs, has_prev, k_shift):
    if has_prev:
        x_ref, y_ref, modp_ref, mod_ref, gain_ref = refs[:5]
        x = x_ref[0] + modp_ref[0, 5:6, :] * y_ref[0]
        rest = refs[5:]
    else:
        x_ref, mod_ref, gain_ref = refs[:3]
        x = x_ref[0]
        rest = refs[3:]
    h = _rms(x) * gain_ref[...]
    h = h * (1.0 + mod_ref[0, k_shift + 1:k_shift + 2, :]) + mod_ref[0, k_shift:k_shift + 1, :]
    return x, h, rest


def _mla_proj_body(has_prev, *refs):
    x, h, rest = _prologue(refs, has_prev, 0)
    (wd_ref, qn_ref, kvn_ref, wq_ref, wqs_ref, wk_ref, wvt_ref, cq_ref, sq_ref, c_ref, s_ref), outs = rest[:11], rest[11:]
    if has_prev:
        x1_ref, q_ref, k_ref, vt_ref = outs
        x1_ref[0] = x
    else:
        q_ref, k_ref, vt_ref = outs
    d = _mm(h.astype(BF16), wd_ref[...])
    cos, sin = c_ref[...], s_ref[...]
    cos_q, sin_q = cq_ref[...], sq_ref[...]
    cq = (_rms(d[:, :MLA_Q_RANK]) * qn_ref[...]).astype(BF16)
    ckv = (_rms(d[:, MLA_Q_RANK:MLA_Q_RANK + MLA_KV_RANK]) * kvn_ref[...]).astype(BF16)
    o = MLA_Q_RANK + MLA_KV_RANK
    pe = d[:, o:o + LANES] * cos + d[:, o + LANES:o + 2 * LANES] * sin
    qa = _mm(cq, wq_ref[...])
    qb = _mm(cq, wqs_ref[...])
    kn = _mm(ckv, wk_ref[...])
    for hd in range(MLA_HEADS):
        sl = slice(hd * LANES, (hd + 1) * LANES)
        q_ref[0, :, sl] = (qa[:, sl] * cos_q + qb[:, sl] * sin_q).astype(BF16)
        k_ref[0, :, sl] = (kn[:, sl] + pe).astype(BF16)
    vt_ref[0] = _nt(wvt_ref[...], ckv).astype(BF16)


def _diff_proj_body(has_prev, *refs):
    x, h, rest = _prologue(refs, has_prev, 0)
    (w_ref, wvt_ref, cq_ref, sq_ref, ck_ref, sk_ref), outs = rest[:6], rest[6:]
    if has_prev:
        x1_ref, q_ref, k_ref, vt_ref = outs
        x1_ref[0] = x
    else:
        q_ref, k_ref, vt_ref = outs
    dm = h.shape[-1]
    hb = h.astype(BF16)
    p = _mm(hb, w_ref[...])
    cq, sq, ck, sk = cq_ref[...], sq_ref[...], ck_ref[...], sk_ref[...]
    for hd in range(dm // LANES):
        sl = slice(hd * LANES, (hd + 1) * LANES)
        q_ref[0, :, sl] = (p[:, sl] * cq + p[:, 2 * dm + hd * LANES:2 * dm + (hd + 1) * LANES] * sq).astype(BF16)
        k_ref[0, :, sl] = (p[:, dm + hd * LANES:dm + (hd + 1) * LANES] * ck
                           + p[:, 3 * dm + hd * LANES:3 * dm + (hd + 1) * LANES] * sk).astype(BF16)
    vt_ref[0] = _nt(wvt_ref[...], hb).astype(BF16)


def _gla_proj_body(has_prev, *refs):
    x, h, rest = _prologue(refs, has_prev, 0)
    (w_ref, wvt_ref, wgu_ref, bg_ref), outs = rest[:4], rest[4:]
    if has_prev:
        x1_ref, q_ref, k_ref, v_ref, vt_ref, g_ref, af_ref, ab_ref = outs
        x1_ref[0] = x
    else:
        q_ref, k_ref, v_ref, vt_ref, g_ref, af_ref, ab_ref = outs
    dm = h.shape[-1]
    dk = dm // 2
    hb = h.astype(BF16)
    p = _mm(hb, w_ref[...])
    q_ref[0] = p[:, :dk] * ((dk // GLA_HEADS) ** -0.5)
    k_ref[0] = p[:, dk:2 * dk]
    v_ref[0] = p[:, 2 * dk:2 * dk + dm].astype(BF16)
    vt_ref[0] = _nt(wvt_ref[...], hb).astype(BF16)
    g_ref[0] = p[:, 2 * dk + dm:2 * dk + 2 * dm]
    z = p[:, 2 * dk + 2 * dm:].astype(BF16)
    a = _mm(z, wgu_ref[...]) + bg_ref[...]
    ls = (jnp.minimum(a, 0.0) - jnp.log(1.0 + jnp.exp(-jnp.abs(a)))) / GLA_GATE_NORM
    af_ref[0] = ls[:, :dk]
    ab_ref[0] = ls[:, dk:]


def _route(logits):
    lane = lax.broadcasted_iota(jnp.int32, logits.shape, 1)

    def col(i):
        return jnp.sum(jnp.where(lane == i, logits, 0.0), axis=-1, keepdims=True)

    lg = [col(i) for i in range(MOE_GROUPS)]
    gmax = functools.reduce(jnp.maximum, lg)
    onehot, taken = [], None
    for i in range(MOE_GROUPS):
        hit = lg[i] == gmax
        if taken is not None:
            hit = jnp.logical_and(hit, jnp.logical_not(taken))
        taken = hit if taken is None else jnp.logical_or(taken, hit)
        onehot.append(hit)
    grp_w = 1.0 / functools.reduce(lambda a, b: a + b, [jnp.exp(l - gmax) for l in lg])
    le = []
    for e in range(MOE_EXPERTS):
        v = None
        for g in range(MOE_GROUPS):
            c = jnp.where(onehot[g], col(MOE_GROUPS + g * MOE_EXPERTS + e), 0.0)
            v = c if v is None else v + c
        le.append(v)
    vmax = functools.reduce(jnp.maximum, le)
    num = []
    for e in range(MOE_EXPERTS):
        rank = None
        for o in range(MOE_EXPERTS):
            if o == e:
                continue
            ahead = (le[o] >= le[e]) if o < e else (le[o] > le[e])
            r = jnp.where(ahead, 1.0, 0.0)
            rank = r if rank is None else rank + r
        num.append(jnp.where(rank < 2.0, jnp.exp(le[e] - vmax), 0.0))
    scale = grp_w / functools.reduce(lambda a, b: a + b, num)
    gates = jnp.zeros(logits.shape, F32)
    for g in range(MOE_GROUPS):
        for e in range(MOE_EXPERTS):
            val = jnp.where(onehot[g], num[e] * scale, 0.0)
            gates = jnp.where(lane == g * MOE_EXPERTS + e, val, gates)
    return gates


def _post_body(kind, *refs):
    x1_ref = refs[0]
    if kind == "gla":
        of_ref, ob_ref, g_ref, hn_ref = refs[1:5]
        rest = refs[5:]
        o = of_ref[0] + ob_ref[0]
        hn = hn_ref[...]
        dv = hn.shape[-1]
        parts = [_rms(o[:, i * dv:(i + 1) * dv]) * hn for i in range(o.shape[-1] // dv)]
        a = (jnp.concatenate(parts, axis=-1) * _silu(g_ref[0])).astype(BF16)
    else:
        a = refs[1][0]
        rest = refs[2:]
    mod_ref, gain_ref, wo_ref, wr_ref, br_ref, x2_ref, h2_ref, gate_ref = rest
    x2 = x1_ref[0] + mod_ref[0, 2:3, :] * _mm(a, wo_ref[...])
    x2_ref[0] = x2
    h2 = _rms(x2) * gain_ref[...]
    h2 = (h2 * (1.0 + mod_ref[0, 4:5, :]) + mod_ref[0, 3:4, :]).astype(BF16)
    h2_ref[0] = h2
    gate_ref[0] = _route(_mm(h2, wr_ref[...]) + br_ref[...])


def _final_body(x_ref, y_ref, mod_ref, gain_ref, o_ref):
    x = x_ref[0] + mod_ref[0, 5:6, :] * y_ref[0]
    o_ref[0] = _rms(x) * gain_ref[...]


def _attn_body(mode, n_lat, n_ctx, tk, lam_init, *refs):
    if mode == "mla":
        q_ref, k_ref, vt_ref, o_ref, s_sc, m_sc, l_sc, acc_sc = refs
    else:
        q_ref, k_ref, vt_ref, lam_ref, sub_ref, o_ref, s_sc, m_sc, l_sc, acc_sc = refs
    tq = q_ref.shape[1]
    dv = acc_sc.shape[1]
    n_steps = n_lat // tk
    is_lat = pl.program_id(2) < (n_lat // tq)

    def q_of(i):
        if mode == "mla":
            return q_ref[0, :, i * LANES:(i + 1) * LANES]
        q = q_ref[0]
        lane = lax.broadcasted_iota(jnp.int32, q.shape, 1)
        keep = (lane < DIFF_HEAD_DIM) if i == 0 else (lane >= DIFF_HEAD_DIM)
        return jnp.where(keep, q, jnp.zeros_like(q))

    qs = [q_of(0), q_of(1)]

    def scores(i, rows):
        k = k_ref[0, rows, i * LANES:(i + 1) * LANES] if mode == "mla" else k_ref[0, rows, :]
        return _nt(k, qs[i])

    def vt_of(i, cols):
        return vt_ref[0, i * dv:(i + 1) * dv, cols] if mode == "mla" else vt_ref[0, :, cols]

    def absorb(i, s, cols, first=False):
        m_new = jnp.max(s, axis=0, keepdims=True)
        if not first:
            m_old = m_sc[i]
            m_new = jnp.maximum(m_old, m_new)
            alpha = jnp.exp2(m_old - m_new)
        p = jnp.exp2(s - m_new)
        l = jnp.sum(p, axis=0, keepdims=True)
        pv = _mm(vt_of(i, cols), p.astype(BF16))
        l_sc[i] = l if first else alpha * l_sc[i] + l
        acc_sc[i] = pv if first else alpha * acc_sc[i] + pv
        m_sc[i] = m_new

    ctx = slice(n_lat, n_lat + n_ctx)
    for i in range(2):
        absorb(i, scores(i, ctx), ctx, first=True)

    @pl.when(is_lat)
    def _():
        def stage(cur, slot, nxt):
            for i in range(2):
                if nxt is not None:
                    s_next = scores(i, pl.ds(pl.multiple_of(nxt, tk), tk))
                absorb(i, s_sc[slot, i], pl.ds(pl.multiple_of(cur, tk), tk))
                if nxt is not None:
                    s_sc[1 - slot, i] = s_next

        for i in range(2):
            s_sc[0, i] = scores(i, slice(0, tk))
        if n_steps % 2 == 0 and n_steps > ATTN_UNROLL:
            def step(j, carry):
                base = 2 * j * tk
                stage(base, 0, base + tk)
                stage(base + tk, 1, base + 2 * tk)
                return carry

            lax.fori_loop(0, n_steps // 2 - 1, step, 0)
            stage((n_steps - 2) * tk, 0, (n_steps - 1) * tk)
            stage((n_steps - 1) * tk, 1, None)
        else:
            for cidx in range(n_steps):
                stage(cidx * tk, cidx % 2, (cidx + 1) * tk if cidx + 1 < n_steps else None)

    o0 = acc_sc[0] * (1.0 / l_sc[0])
    o1 = acc_sc[1] * (1.0 / l_sc[1])
    if mode == "mla":
        o = jnp.concatenate([o0, o1], axis=0)
    else:
        lam = lam_ref[...]
        lam_full = (jnp.exp(jnp.sum(lam[0:1] * lam[1:2], axis=-1, keepdims=True))
                    - jnp.exp(jnp.sum(lam[2:3] * lam[3:4], axis=-1, keepdims=True)) + lam_init)
        o = o0 - lam_full * o1
        o = o * lax.rsqrt(jnp.mean(o * o, axis=0, keepdims=True) + EPS) * sub_ref[...] * (1.0 - lam_init)
    o_ref[0] = o.T.astype(BF16)


def _attention(mode, q, k, vt, n_lat, n_ctx, extra=(), lam_init=0.0):
    B, T, _ = q.shape
    tq = ROW_TILE
    tk = math.gcd(n_lat, ATTN_KEYS)
    qw = 2 * LANES if mode == "mla" else LANES
    dv = MLA_V if mode == "mla" else LANES
    nh = q.shape[-1] // qw
    in_specs = [pl.BlockSpec((1, tq, qw), lambda b, h, i: (b, i, h)),
                pl.BlockSpec((1, T, qw), lambda b, h, i: (b, 0, h)),
                pl.BlockSpec((1, LANES, T), lambda b, h, i: (b, h, 0))]
    for a in extra:
        in_specs.append(pl.BlockSpec(a.shape, lambda b, h, i, nd=a.ndim: (0,) * nd))
    return pl.pallas_call(
        functools.partial(_attn_body, mode, n_lat, n_ctx, tk, lam_init),
        grid=(B, nh, T // tq),
        in_specs=in_specs,
        out_specs=pl.BlockSpec((1, tq, LANES), lambda b, h, i: (b, i, h)),
        out_shape=jax.ShapeDtypeStruct((B, T, nh * LANES), BF16),
        scratch_shapes=[pltpu.VMEM((2, 2, tk, tq), F32), pltpu.VMEM((2, 1, tq), F32),
                        pltpu.VMEM((2, 1, tq), F32), pltpu.VMEM((2, dv, tq), F32)],
        compiler_params=_cp("parallel", "parallel", "arbitrary"),
        name="attn_" + mode,
    )(q, k, vt, *extra)


def _cumsum_rows(tri, x):
    hi = x.astype(BF16)
    r1 = x - hi.astype(F32)
    mid = r1.astype(BF16)
    lo = (r1 - mid.astype(F32)).astype(BF16)
    return _mm(tri, hi) + _mm(tri, mid) + _mm(tri, lo)


def _gla_scan_body(n_chunks, *refs):
    (qf_ref, kf_ref, vf_ref, vtf_ref, af_ref, qb_ref, kb_ref, vb_ref, vtb_ref, ab_ref,
     of_ref, ob_ref, sf_sc, sb_sc) = refs
    C = GLA_CHUNK

    @pl.when(pl.program_id(2) == 0)
    def _():
        sf_sc[...] = jnp.zeros_like(sf_sc)
        sb_sc[...] = jnp.zeros_like(sb_sc)

    r = lax.broadcasted_iota(jnp.int32, (C, C), 0)
    cidx = lax.broadcasted_iota(jnp.int32, (C, C), 1)
    lower = cidx <= r

    def chunk(ci, q_ref, k_ref, v_ref, vt_ref, a_ref, o_ref, s_sc, reverse):
        rows = slice(ci * C, (ci + 1) * C)
        keep = jnp.logical_not(lower) | (cidx == r) if reverse else lower
        tri = jnp.where(keep, 1.0, 0.0).astype(BF16)
        cum = _cumsum_rows(tri, a_ref[0, rows, :])
        total = cum[0:1, :] if reverse else cum[C - 1:C, :]
        q = q_ref[0, rows, :]
        k = k_ref[0, rows, :]
        qd = (q * jnp.exp(cum)).astype(BF16)
        ki = (k * jnp.exp(-cum)).astype(BF16)
        kd = (k * jnp.exp(total - cum)).astype(BF16)
        att = jnp.where(keep, _nt(qd, ki), 0.0).astype(BF16)
        st = s_sc[...]
        o_ref[0, rows, :] = _nt(qd, st.astype(BF16)) + _mm(att, v_ref[0, rows, :])
        s_sc[...] = st * jnp.exp(total) + _mm(vt_ref[0, :, rows], kd)

    for ci in range(n_chunks):
        chunk(ci, qf_ref, kf_ref, vf_ref, vtf_ref, af_ref, of_ref, sf_sc, False)
        chunk(n_chunks - 1 - ci, qb_ref, kb_ref, vb_ref, vtb_ref, ab_ref, ob_ref, sb_sc, True)


def _gla_scan(q, k, v, vt, af, ab, n_lat):
    B, T, dk = q.shape
    dm = v.shape[-1]
    dkh, dvh = dk // GLA_HEADS, dm // GLA_HEADS
    lb = 2 * GLA_CHUNK
    nb = T // lb
    n_lat_b = n_lat // lb
    n_ctx_b = nb - n_lat_b

    def fwd(j):
        return jnp.where(j < n_ctx_b, n_lat_b + j, j - n_ctx_b)

    def bwd(j):
        return jnp.where(j < n_ctx_b, nb - 1 - j, nb - 1 - j)

    def specs(blk):
        return [pl.BlockSpec((1, lb, dkh), lambda b, h, j: (b, blk(j), h)),
                pl.BlockSpec((1, lb, dkh), lambda b, h, j: (b, blk(j), h)),
                pl.BlockSpec((1, lb, dvh), lambda b, h, j: (b, blk(j), h)),
                pl.BlockSpec((1, dvh, lb), lambda b, h, j: (b, h, blk(j))),
                pl.BlockSpec((1, lb, dkh), lambda b, h, j: (b, blk(j), h))]

    o_sds = jax.ShapeDtypeStruct((B, T, dm), F32)
    return pl.pallas_call(
        functools.partial(_gla_scan_body, lb // GLA_CHUNK),
        grid=(B, GLA_HEADS, nb),
        in_specs=specs(fwd) + specs(bwd),
        out_specs=[pl.BlockSpec((1, lb, dvh), lambda b, h, j: (b, fwd(j), h)),
                   pl.BlockSpec((1, lb, dvh), lambda b, h, j: (b, bwd(j), h))],
        out_shape=[o_sds, o_sds],
        scratch_shapes=[pltpu.VMEM((dvh, dkh), F32), pltpu.VMEM((dvh, dkh), F32)],
        compiler_params=_cp("parallel", "parallel", "arbitrary"),
        name="gla_scan",
    )(q, k, v, vt, af, q, k, v, vt, ab)


def _moe_body(ne, h_ref, g_ref, tri_ref, wg_ref, wu_ref, wdt_ref, y_ref, key_sc, gt_sc, yt_sc):
    e = pl.program_id(1)
    tm = h_ref.shape[0]
    C = MOE_SLOTS

    @pl.when(e == 0)
    def _():
        gt = g_ref[...].T[:ne]
        sel = gt != 0.0
        rank = _mm(jnp.where(sel, 1.0, 0.0).astype(BF16), tri_ref[...])
        key_sc[...] = jnp.where(sel, rank, -1.0)
        gt_sc[...] = gt
        yt_sc[...] = jnp.zeros_like(yt_sc)

    keys = [key_sc[pl.ds(e * MOE_PAIR + k, 1), :] for k in range(MOE_PAIR)]
    gates = [gt_sc[pl.ds(e * MOE_PAIR + k, 1), :] for k in range(MOE_PAIR)]

    def select(k, r):
        slot = (lax.broadcasted_iota(jnp.int32, (C, tm), 0) + r * C).astype(F32)
        hit = keys[k] == slot
        return jnp.where(hit, 1.0, 0.0).astype(BF16), jnp.where(hit, gates[k], 0.0).astype(BF16)

    def expert(k, xg):
        a = _mm(xg, wg_ref[k])
        u = _mm(xg, wu_ref[k])
        act = (_silu(a) * u).astype(BF16)
        return _nt(wdt_ref[k], act).astype(BF16)

    def run(ks, r):
        sel = [select(k, r) for k in ks]
        xg = _mm(jnp.concatenate([p for p, _ in sel], axis=0), h_ref[...]).astype(BF16)
        zt = [expert(k, xg[i * C:(i + 1) * C]) for i, k in enumerate(ks)]
        yt_sc[...] += _mm(jnp.concatenate(zt, axis=1), jnp.concatenate([pg for _, pg in sel], axis=0))

    run(list(range(MOE_PAIR)), 0)
    for k in range(MOE_PAIR):
        n_chunks = (jnp.max(keys[k]).astype(jnp.int32) + C) // C

        def more(r, carry, k=k):
            run([k], r)
            return carry

        lax.fori_loop(1, n_chunks, more, 0)

    @pl.when(e == ne // MOE_PAIR - 1)
    def _():
        y_ref[...] = yt_sc[...].T


def _moe(h2, gates, wg, wu, wdt):
    n, d = h2.shape
    ne, _, f = wg.shape
    tm = math.gcd(n, MOE_TILE)
    tri = jnp.triu(jnp.ones((tm, tm), F32), 1).astype(BF16)
    return pl.pallas_call(
        functools.partial(_moe_body, ne),
        grid=(n // tm, ne // MOE_PAIR),
        in_specs=[pl.BlockSpec((tm, d), lambda t, e: (t, 0)),
                  pl.BlockSpec((tm, LANES), lambda t, e: (t, 0)),
                  pl.BlockSpec((tm, tm), lambda t, e: (0, 0)),
                  pl.BlockSpec((MOE_PAIR, d, f), lambda t, e: (e, 0, 0)),
                  pl.BlockSpec((MOE_PAIR, d, f), lambda t, e: (e, 0, 0)),
                  pl.BlockSpec((MOE_PAIR, d, f), lambda t, e: (e, 0, 0))],
        out_specs=pl.BlockSpec((tm, d), lambda t, e: (t, 0)),
        out_shape=jax.ShapeDtypeStruct((n, d), F32),
        scratch_shapes=[pltpu.VMEM((ne, tm), F32), pltpu.VMEM((ne, tm), F32), pltpu.VMEM((d, tm), F32)],
        compiler_params=_cp("parallel", "arbitrary"),
        name="moe",
    )(h2, gates, tri, wg, wu, wdt)


def _rope_angles(n_lat, n_ctx, rot_dim):
    rows = n_lat // GRID_W
    row = jnp.repeat(jnp.arange(rows, dtype=F32), GRID_W)
    col = jnp.tile(jnp.arange(GRID_W, dtype=F32), rows)
    n_freq = rot_dim // 4
    inv_freq = ROPE_BASE ** (-jnp.arange(n_freq, dtype=F32) / n_freq)
    ang = jnp.concatenate([row[:, None] * inv_freq, col[:, None] * inv_freq], axis=-1)
    cos = jnp.concatenate([jnp.cos(ang), jnp.ones((n_ctx, rot_dim // 2), F32)], axis=0)
    sin = jnp.concatenate([jnp.sin(ang), jnp.zeros((n_ctx, rot_dim // 2), F32)], axis=0)
    return cos, sin


def _swap_halves(w, width):
    lead = w.shape[:-1]
    w = w.reshape(*lead, -1, 2, width // 2)
    return jnp.flip(w, axis=-2).reshape(*lead, -1)


def kernel(x, c, ctx, c_ctx, w_ada, b_ada, norm_mix, norm_ffn, mla_w_dqkv, mla_q_norm, mla_w_uq, mla_kv_norm, mla_w_ukv, mla_w_o, diff_w_qkv, diff_lambda, diff_subln, diff_w_o, gla_w_in, gla_w_gate_up, gla_b_gate, gla_head_norm, gla_w_o, moe_w_group, moe_b_group, moe_w_router, moe_b_router, moe_w_gate, moe_w_up, moe_w_down, final_norm):
    B, n_lat, D = x.shape
    n_ctx = ctx.shape[1]
    T = n_lat + n_ctx
    depth = w_ada.shape[0]
    assert n_lat % ROW_TILE == 0 and n_ctx % ROW_TILE == 0 and B + 1 <= 8 and D % LANES == 0
    R = _Rows(B, T, n_lat)

    cvec = jnp.zeros((8, D), F32).at[:B].set(c).at[B].set(c_ctx)
    mods = _ada_all(cvec, w_ada, b_ada).reshape(depth, 8, N_MOD, D)

    stream = jnp.concatenate([x, ctx], axis=1)
    y_prev = None

    cos_a, sin_a = _rope_angles(n_lat, n_ctx, MLA_ROPE)
    ones, zeros = jnp.ones((T, MLA_NOPE), F32), jnp.zeros((T, LANES - MLA_NOPE - MLA_ROPE), F32)
    mla_cos = jnp.concatenate([ones, cos_a, cos_a, zeros], axis=1)
    mla_sin = jnp.concatenate([0 * ones, -sin_a, sin_a, zeros], axis=1)
    cos_b, sin_b = _rope_angles(n_lat, n_ctx, DIFF_HEAD_DIM)
    diff_cos = jnp.concatenate([cos_b] * 4, axis=1)
    diff_sin = jnp.concatenate([-sin_b, sin_b] * 2, axis=1)
    diff_scale = DIFF_HEAD_DIM ** -0.5 * math.log2(math.e)
    mla_qscale = (MLA_NOPE + MLA_ROPE) ** -0.5 * math.log2(math.e)

    for li in range(depth):
        last = li == depth - 1
        kind, j = li % 3, li // 3
        mod = mods[li]
        has_prev = y_prev is not None
        head = [stream] + ([y_prev, mods[li - 1]] if has_prev else []) + [mod, norm_mix[li].reshape(1, D)]
        head_specs = [R.rows(D)] + ([R.rows(D), R.mod(D)] if has_prev else []) + [R.mod(D), R.full(head[-1])]
        x1_out = ([R.out(D, F32)], [R.rows(D)]) if has_prev else ([], [])

        if kind == 0:
            wd = mla_w_dqkv[j]
            pe_w = wd[:, MLA_Q_RANK + MLA_KV_RANK:]
            padl, padr = jnp.zeros((D, MLA_NOPE), F32), jnp.zeros((D, LANES - MLA_NOPE - MLA_ROPE), F32)
            wdown = jnp.concatenate([wd[:, :MLA_Q_RANK + MLA_KV_RANK], padl, pe_w, padr,
                                     padl, _swap_halves(pe_w, MLA_ROPE), padr], axis=1).astype(BF16)
            wq = mla_w_uq[j].reshape(MLA_Q_RANK, MLA_HEADS, MLA_NOPE + MLA_ROPE)
            pad = jnp.zeros((MLA_Q_RANK, MLA_HEADS, LANES - MLA_NOPE - MLA_ROPE), F32)
            wq_a = jnp.concatenate([wq, pad], axis=-1).reshape(MLA_Q_RANK, -1).astype(BF16)
            wq_s = jnp.concatenate([0 * wq[..., :MLA_NOPE], _swap_halves(wq[..., MLA_NOPE:], MLA_ROPE), pad],
                                   axis=-1).reshape(MLA_Q_RANK, -1).astype(BF16)
            wkv = mla_w_ukv[j].reshape(MLA_KV_RANK, MLA_HEADS, MLA_NOPE + MLA_V)
            wk = jnp.concatenate([wkv[..., :MLA_NOPE], jnp.zeros((MLA_KV_RANK, MLA_HEADS, LANES - MLA_NOPE), F32)],
                                 axis=-1).reshape(MLA_KV_RANK, -1).astype(BF16)
            wvt = wkv[..., MLA_NOPE:].reshape(MLA_KV_RANK, -1).T.astype(BF16)
            ins = head + [wdown, mla_q_norm[j].reshape(1, -1), mla_kv_norm[j].reshape(1, -1),
                          wq_a, wq_s, wk, wvt, mla_cos * mla_qscale, mla_sin * mla_qscale, mla_cos, mla_sin]
            specs = head_specs + [R.full(a) for a in ins[len(head):-4]] + [R.table(LANES)] * 4
            outs = pl.pallas_call(
                functools.partial(_mla_proj_body, has_prev), grid=R.grid, in_specs=specs,
                out_specs=x1_out[1] + [R.rows(MLA_HEADS * LANES), R.rows(MLA_HEADS * LANES), R.cols(MLA_HEADS * MLA_V)],
                out_shape=x1_out[0] + [R.out(MLA_HEADS * LANES, BF16), R.out(MLA_HEADS * LANES, BF16),
                                       jax.ShapeDtypeStruct((B, MLA_HEADS * MLA_V, T), BF16)],
                compiler_params=_cp("parallel", "parallel"), name="mla_proj")(*ins)
            if has_prev:
                stream = outs[0]
            q, k, vt = outs[-3:]
            mix = [_attention("mla", q, k, vt, n_lat, n_ctx)]
            mix_specs = [R.rows(D)]
            w_o = mla_w_o[j]
        elif kind == 1:
            lam_init = 0.8 - 0.6 * math.exp(-0.3 * li)
            wq, wk, wv = jnp.split(diff_w_qkv[j], 3, axis=1)
            w = jnp.concatenate([wq, wk, _swap_halves(wq, DIFF_HEAD_DIM), _swap_halves(wk, DIFF_HEAD_DIM)],
                                axis=1).astype(BF16)
            wvt = wv.T.astype(BF16)
            ins = head + [w, wvt, diff_cos * diff_scale, diff_sin * diff_scale, diff_cos, diff_sin]
            specs = head_specs + [R.full(w), R.full(wvt)] + [R.table(LANES)] * 4
            outs = pl.pallas_call(
                functools.partial(_diff_proj_body, has_prev), grid=R.grid, in_specs=specs,
                out_specs=x1_out[1] + [R.rows(D), R.rows(D), R.cols(D)],
                out_shape=x1_out[0] + [R.out(D, BF16), R.out(D, BF16), jax.ShapeDtypeStruct((B, D, T), BF16)],
                compiler_params=_cp("parallel", "parallel"), name="diff_proj")(*ins)
            if has_prev:
                stream = outs[0]
            q, k, vt = outs[-3:]
            mix = [_attention("diff", q, k, vt, n_lat, n_ctx,
                              extra=(diff_lambda[j], diff_subln[j].reshape(-1, 1)), lam_init=lam_init)]
            mix_specs = [R.rows(D)]
            w_o = diff_w_o[j]
        else:
            dk = D // 2
            win = gla_w_in[j]
            zpad = jnp.zeros((D, LANES - 2 * GLA_GATE_RANK), F32)
            w = jnp.concatenate([win, zpad], axis=1).astype(BF16)
            wvt = win[:, 2 * dk:2 * dk + D].T.astype(BF16)
            wgu = jnp.zeros((LANES, 2 * dk), F32)
            wgu = wgu.at[:GLA_GATE_RANK, :dk].set(gla_w_gate_up[j, 0])
            wgu = wgu.at[GLA_GATE_RANK:2 * GLA_GATE_RANK, dk:].set(gla_w_gate_up[j, 1]).astype(BF16)
            bg = gla_b_gate[j].reshape(1, 2 * dk)
            ins = head + [w, wvt, wgu, bg]
            specs = head_specs + [R.full(a) for a in ins[len(head):]]
            vt_spec = pl.BlockSpec((1, D, R.tm), lambda b, t: (b, 0, t))
            outs = pl.pallas_call(
                functools.partial(_gla_proj_body, has_prev), grid=R.grid, in_specs=specs,
                out_specs=x1_out[1] + [R.rows(dk), R.rows(dk), R.rows(D), vt_spec, R.rows(D), R.rows(dk), R.rows(dk)],
                out_shape=x1_out[0] + [R.out(dk, F32), R.out(dk, F32), R.out(D, BF16),
                                       jax.ShapeDtypeStruct((B, D, T), BF16), R.out(D, F32),
                                       R.out(dk, F32), R.out(dk, F32)],
                compiler_params=_cp("parallel", "parallel"), name="gla_proj")(*ins)
            if has_prev:
                stream = outs[0]
            q, k, v, vt, g, af, ab = outs[-7:]
            o_f, o_b = _gla_scan(q, k, v, vt, af, ab, n_lat)
            hn = gla_head_norm[j].reshape(1, -1)
            mix = [o_f, o_b, g, hn]
            mix_specs = [R.rows(D), R.rows(D), R.rows(D), R.full(hn)]
            w_o = gla_w_o[j]

        wr = jnp.concatenate([moe_w_group[li], jnp.moveaxis(moe_w_router[li], 0, 1).reshape(D, -1)], axis=1)
        n_r = wr.shape[1]
        wr = jnp.concatenate([wr, jnp.zeros((D, LANES - n_r), F32)], axis=1).astype(BF16)
        br = jnp.concatenate([moe_b_group[li], moe_b_router[li].reshape(-1), jnp.zeros((LANES - n_r,), F32)]).reshape(1, LANES)
        tail = [mod, norm_ffn[li].reshape(1, D), w_o.astype(BF16), wr, br]
        stream, h2, gates = pl.pallas_call(
            functools.partial(_post_body, "gla" if kind == 2 else "attn"), grid=R.grid,
            in_specs=[R.rows(D)] + mix_specs + [R.mod(D)] + [R.full(a) for a in tail[1:]],
            out_specs=[R.rows(D), R.rows(D), R.rows(LANES)],
            out_shape=[R.out(D, F32), R.out(D, BF16), R.out(LANES, F32)],
            compiler_params=_cp("parallel", "parallel"), name="post_mixer")(stream, *mix, *tail)

        ne = MOE_GROUPS * MOE_EXPERTS
        f = moe_w_gate.shape[-1]
        y = _moe(h2.reshape(B * T, D), gates.reshape(B * T, LANES),
                 moe_w_gate[li].reshape(ne, D, f).astype(BF16), moe_w_up[li].reshape(ne, D, f).astype(BF16),
                 jnp.swapaxes(moe_w_down[li].reshape(ne, f, D), 1, 2).astype(BF16))
        y_prev = y.reshape(B, T, D)

    RL = _Rows(B, n_lat, n_lat)
    fin = final_norm.reshape(1, D)
    return pl.pallas_call(
        _final_body, grid=RL.grid,
        in_specs=[RL.rows(D), RL.rows(D), RL.mod(D), RL.full(fin)],
        out_specs=RL.rows(D), out_shape=RL.out(D, F32),
        compiler_params=_cp("parallel", "parallel"), name="final_norm")(stream, y_prev, mods[depth - 1], fin)
```

```python
import functools
import math

import jax
import jax.numpy as jnp
from jax import lax
from jax.experimental import pallas as pl
from jax.experimental.pallas import tpu as pltpu

F32 = jnp.float32
BF16 = jnp.bfloat16

EPS = 1e-6
ROPE_BASE = 10000.0
GRID_W = 64
N_MOD = 6
LANES = 128

MLA_HEADS = 16
MLA_Q_RANK = 256
MLA_KV_RANK = 256
MLA_NOPE = 64
MLA_ROPE = 32
MLA_V = 64
DIFF_HEAD_DIM = 64
GLA_HEADS = 4
GLA_GATE_RANK = 16
GLA_GATE_NORM = 16.0
GLA_CHUNK = 128
MOE_GROUPS = 4
MOE_EXPERTS = 4

ROW_TILE = 256
MOE_TILE = 1024
MOE_SLOTS = 256
MOE_PAIR = 2
ATTN_KEYS = 1024
VMEM_LIMIT = 52 * 1024 * 1024


def _cp(*sem):
    return pltpu.CompilerParams(dimension_semantics=sem, vmem_limit_bytes=VMEM_LIMIT)


def _mm(a, b):
    return jnp.dot(a, b, preferred_element_type=F32)


def _nt(a, b):
    return lax.dot_general(a, b, (((1,), (1,)), ((), ())), preferred_element_type=F32)


def _rms(x):
    return x * lax.rsqrt(jnp.mean(x * x, axis=-1, keepdims=True) + EPS)


def _silu(x):
    return x * (1.0 / (1.0 + jnp.exp(-x)))


def _ada_body(c_ref, w_ref, b_ref, o_ref):
    c = c_ref[...]
    o_ref[0] = _mm(_silu(c).astype(BF16), w_ref[0].astype(BF16)) + b_ref[0]


def _ada_all(cvec, w_ada, b_ada):
    depth, d, n6 = w_ada.shape
    tn = 1536
    return pl.pallas_call(
        _ada_body,
        grid=(depth, n6 // tn),
        in_specs=[pl.BlockSpec((8, d), lambda l, j: (0, 0)),
                  pl.BlockSpec((1, d, tn), lambda l, j: (l, 0, j)),
                  pl.BlockSpec((1, 1, tn), lambda l, j: (l, 0, j))],
        out_specs=pl.BlockSpec((1, 8, tn), lambda l, j: (l, 0, j)),
        out_shape=jax.ShapeDtypeStruct((depth, 8, n6), F32),
        compiler_params=_cp("parallel", "parallel"),
        name="adaln",
    )(cvec, w_ada, b_ada.reshape(depth, 1, n6))


class _Rows:
    def __init__(self, B, T, n_lat, tm=ROW_TILE):
        self.B, self.T, self.tm = B, T, tm
        self.n_lat_tiles = n_lat // tm
        self.grid = (B, T // tm)

    def rows(self, c):
        return pl.BlockSpec((1, self.tm, c), lambda b, t: (b, t, 0))

    def mod(self, d):
        nl, B = self.n_lat_tiles, self.B
        return pl.BlockSpec((1, N_MOD, d), lambda b, t: (jnp.where(t < nl, b, B), 0, 0))

    def cols(self, c):
        return pl.BlockSpec((1, c, self.tm), lambda b, t: (b, 0, t))

    def table(self, c):
        return pl.BlockSpec((self.tm, c), lambda b, t: (t, 0))

    @staticmethod
    def full(a):
        nd = a.ndim
        return pl.BlockSpec(a.shape, lambda b, t: (0,) * nd)

    def out(self, c, dtype):
        return jax.ShapeDtypeStruct((self.B, self.T, c), dtype)


def _prologue(refs, has_prev, k_shift):
    if has_prev:
        x_ref, y_ref, modp_ref, mod_ref, gain_ref = refs[:5]
        x = x_ref[0] + modp_ref[0, 5:6, :] * y_ref[0]
        rest = refs[5:]
    else:
        x_ref, mod_ref, gain_ref = refs[:3]
        x = x_ref[0]
        rest = refs[3:]
    h = _rms(x) * gain_ref[...]
    h = h * (1.0 + mod_ref[0, k_shift + 1:k_shift + 2, :]) + mod_ref[0, k_shift:k_shift + 1, :]
    return x, h, rest


def _mla_proj_body(has_prev, *refs):
    x, h, rest = _prologue(refs, has_prev, 0)
    (wd_ref, qn_ref, kvn_ref, wq_ref, wqs_ref, wk_ref, wvt_ref, cq_ref, sq_ref, c_ref, s_ref), outs = rest[:11], rest[11:]
    if has_prev:
        x1_ref, q_ref, k_ref, vt_ref = outs
        x1_ref[0] = x
    else:
        q_ref, k_ref, vt_ref = outs
    d = _mm(h.astype(BF16), wd_ref[...])
    cos, sin = c_ref[...], s_ref[...]
    cos_q, sin_q = cq_ref[...], sq_ref[...]
    cq = (_rms(d[:, :MLA_Q_RANK]) * qn_ref[...]).astype(BF16)
    ckv = (_rms(d[:, MLA_Q_RANK:MLA_Q_RANK + MLA_KV_RANK]) * kvn_ref[...]).astype(BF16)
    o = MLA_Q_RANK + MLA_KV_RANK
    pe = d[:, o:o + LANES] * cos + d[:, o + LANES:o + 2 * LANES] * sin
    qa = _mm(cq, wq_ref[...])
    qb = _mm(cq, wqs_ref[...])
    kn = _mm(ckv, wk_ref[...])
    for hd in range(MLA_HEADS):
        sl = slice(hd * LANES, (hd + 1) * LANES)
        q_ref[0, :, sl] = (qa[:, sl] * cos_q + qb[:, sl] * sin_q).astype(BF16)
        k_ref[0, :, sl] = (kn[:, sl] + pe).astype(BF16)
    vt_ref[0] = _nt(wvt_ref[...], ckv).astype(BF16)


def _diff_proj_body(has_prev, *refs):
    x, h, rest = _prologue(refs, has_prev, 0)
    (w_ref, wvt_ref, cq_ref, sq_ref, ck_ref, sk_ref), outs = rest[:6], rest[6:]
    if has_prev:
        x1_ref, q_ref, k_ref, vt_ref = outs
        x1_ref[0] = x
    else:
        q_ref, k_ref, vt_ref = outs
    dm = h.shape[-1]
    hb = h.astype(BF16)
    p = _mm(hb, w_ref[...])
    cq, sq, ck, sk = cq_ref[...], sq_ref[...], ck_ref[...], sk_ref[...]
    for hd in range(dm // LANES):
        sl = slice(hd * LANES, (hd + 1) * LANES)
        q_ref[0, :, sl] = (p[:, sl] * cq + p[:, 2 * dm + hd * LANES:2 * dm + (hd + 1) * LANES] * sq).astype(BF16)
        k_ref[0, :, sl] = (p[:, dm + hd * LANES:dm + (hd + 1) * LANES] * ck
                           + p[:, 3 * dm + hd * LANES:3 * dm + (hd + 1) * LANES] * sk).astype(BF16)
    vt_ref[0] = _nt(wvt_ref[...], hb).astype(BF16)


def _gla_proj_body(has_prev, *refs):
    x, h, rest = _prologue(refs, has_prev, 0)
    (w_ref, wvt_ref, wgu_ref, bg_ref), outs = rest[:4], rest[4:]
    if has_prev:
        x1_ref, q_ref, k_ref, v_ref, vt_ref, g_ref, af_ref, ab_ref = outs
        x1_ref[0] = x
    else:
        q_ref, k_ref, v_ref, vt_ref, g_ref, af_ref, ab_ref = outs
    dm = h.shape[-1]
    dk = dm // 2
    hb = h.astype(BF16)
    p = _mm(hb, w_ref[...])
    q_ref[0] = p[:, :dk] * ((dk // GLA_HEADS) ** -0.5)
    k_ref[0] = p[:, dk:2 * dk]
    v_ref[0] = p[:, 2 * dk:2 * dk + dm].astype(BF16)
    vt_ref[0] = _nt(wvt_ref[...], hb).astype(BF16)
    g_ref[0] = p[:, 2 * dk + dm:2 * dk + 2 * dm]
    z = p[:, 2 * dk + 2 * dm:].astype(BF16)
    a = _mm(z, wgu_ref[...]) + bg_ref[...]
    ls = (jnp.minimum(a, 0.0) - jnp.log(1.0 + jnp.exp(-jnp.abs(a)))) / GLA_GATE_NORM
    af_ref[0] = ls[:, :dk]
    ab_ref[0] = ls[:, dk:]


def _route(logits):
    lane = lax.broadcasted_iota(jnp.int32, logits.shape, 1)

    def col(i):
        return jnp.sum(jnp.where(lane == i, logits, 0.0), axis=-1, keepdims=True)

    lg = [col(i) for i in range(MOE_GROUPS)]
    gmax = functools.reduce(jnp.maximum, lg)
    onehot, taken = [], None
    for i in range(MOE_GROUPS):
        hit = lg[i] == gmax
        if taken is not None:
            hit = jnp.logical_and(hit, jnp.logical_not(taken))
        taken = hit if taken is None else jnp.logical_or(taken, hit)
        onehot.append(hit)
    grp_w = 1.0 / functools.reduce(lambda a, b: a + b, [jnp.exp(l - gmax) for l in lg])
    le = []
    for e in range(MOE_EXPERTS):
        v = None
        for g in range(MOE_GROUPS):
            c = jnp.where(onehot[g], col(MOE_GROUPS + g * MOE_EXPERTS + e), 0.0)
            v = c if v is None else v + c
        le.append(v)
    vmax = functools.reduce(jnp.maximum, le)
    num = []
    for e in range(MOE_EXPERTS):
        rank = None
        for o in range(MOE_EXPERTS):
            if o == e:
                continue
            ahead = (le[o] >= le[e]) if o < e else (le[o] > le[e])
            r = jnp.where(ahead, 1.0, 0.0)
            rank = r if rank is None else rank + r
        num.append(jnp.where(rank < 2.0, jnp.exp(le[e] - vmax), 0.0))
    scale = grp_w / functools.reduce(lambda a, b: a + b, num)
    gates = jnp.zeros(logits.shape, F32)
    for g in range(MOE_GROUPS):
        for e in range(MOE_EXPERTS):
            val = jnp.where(onehot[g], num[e] * scale, 0.0)
            gates = jnp.where(lane == g * MOE_EXPERTS + e, val, gates)
    return gates


def _post_body(kind, *refs):
    x1_ref = refs[0]
    if kind == "gla":
        of_ref, ob_ref, g_ref, hn_ref = refs[1:5]
        rest = refs[5:]
        o = of_ref[0] + ob_ref[0]
        hn = hn_ref[...]
        dv = hn.shape[-1]
        parts = [_rms(o[:, i * dv:(i + 1) * dv]) * hn for i in range(o.shape[-1] // dv)]
        a = (jnp.concatenate(parts, axis=-1) * _silu(g_ref[0])).astype(BF16)
    else:
        a = refs[1][0]
        rest = refs[2:]
    mod_ref, gain_ref, wo_ref, wr_ref, br_ref, x2_ref, h2_ref, gate_ref = rest
    x2 = x1_ref[0] + mod_ref[0, 2:3, :] * _mm(a, wo_ref[...])
    x2_ref[0] = x2
    h2 = _rms(x2) * gain_ref[...]
    h2 = (h2 * (1.0 + mod_ref[0, 4:5, :]) + mod_ref[0, 3:4, :]).astype(BF16)
    h2_ref[0] = h2
    gate_ref[0] = _route(_mm(h2, wr_ref[...]) + br_ref[...])


def _final_body(x_ref, y_ref, mod_ref, gain_ref, o_ref):
    x = x_ref[0] + mod_ref[0, 5:6, :] * y_ref[0]
    o_ref[0] = _rms(x) * gain_ref[...]


def _attn_body(mode, n_lat, n_ctx, tk, lam_init, *refs):
    if mode == "mla":
        q_ref, k_ref, vt_ref, o_ref, s_sc, m_sc, l_sc, acc_sc = refs
    else:
        q_ref, k_ref, vt_ref, lam_ref, sub_ref, o_ref, s_sc, m_sc, l_sc, acc_sc = refs
    tq = q_ref.shape[1]
    dv = acc_sc.shape[1]
    n_steps = (n_lat + n_ctx) // tk
    is_lat = pl.program_id(2) < (n_lat // tq)

    def q_of(i):
        if mode == "mla":
            return q_ref[0, :, i * LANES:(i + 1) * LANES]
        q = q_ref[0]
        lane = lax.broadcasted_iota(jnp.int32, q.shape, 1)
        keep = (lane < DIFF_HEAD_DIM) if i == 0 else (lane >= DIFF_HEAD_DIM)
        return jnp.where(keep, q, jnp.zeros_like(q))

    qs = [q_of(0), q_of(1)]

    def scores(i, rows):
        k = k_ref[0, rows, i * LANES:(i + 1) * LANES] if mode == "mla" else k_ref[0, rows, :]
        return _nt(k, qs[i])

    def vt_of(i, cols):
        return vt_ref[0, i * dv:(i + 1) * dv, cols] if mode == "mla" else vt_ref[0, :, cols]

    def absorb(i, s, cols, first=False):
        m_new = jnp.max(s, axis=0, keepdims=True)
        if not first:
            m_old = m_sc[i]
            m_new = jnp.maximum(m_old, m_new)
            alpha = jnp.exp2(m_old - m_new)
        p = jnp.exp2(s - m_new)
        l = jnp.sum(p, axis=0, keepdims=True)
        pv = _mm(vt_of(i, cols), p.astype(BF16))
        l_sc[i] = l if first else alpha * l_sc[i] + l
        acc_sc[i] = pv if first else alpha * acc_sc[i] + pv
        m_sc[i] = m_new

    @pl.when(jnp.logical_not(is_lat))
    def _():
        ctx = slice(n_lat, n_lat + n_ctx)
        for i in range(2):
            absorb(i, scores(i, ctx), ctx, first=True)

    @pl.when(is_lat)
    def _():
        for i in range(2):
            s_sc[0, i] = scores(i, slice(0, tk))
        for c in range(n_steps):
            slot = c % 2
            for i in range(2):
                if c + 1 < n_steps:
                    s_next = scores(i, slice((c + 1) * tk, (c + 2) * tk))
                absorb(i, s_sc[slot, i], slice(c * tk, (c + 1) * tk), first=(c == 0))
                if c + 1 < n_steps:
                    s_sc[1 - slot, i] = s_next

    o0 = acc_sc[0] * (1.0 / l_sc[0])
    o1 = acc_sc[1] * (1.0 / l_sc[1])
    if mode == "mla":
        o = jnp.concatenate([o0, o1], axis=0)
    else:
        lam = lam_ref[...]
        lam_full = (jnp.exp(jnp.sum(lam[0:1] * lam[1:2], axis=-1, keepdims=True))
                    - jnp.exp(jnp.sum(lam[2:3] * lam[3:4], axis=-1, keepdims=True)) + lam_init)
        o = o0 - lam_full * o1
        o = o * lax.rsqrt(jnp.mean(o * o, axis=0, keepdims=True) + EPS) * sub_ref[...] * (1.0 - lam_init)
    o_ref[0] = o.T.astype(BF16)


def _attention(mode, q, k, vt, n_lat, n_ctx, extra=(), lam_init=0.0):
    B, T, _ = q.shape
    tq = ROW_TILE
    tk = max(t for t in range(256, ATTN_KEYS + 1, 256) if T % t == 0)
    qw = 2 * LANES if mode == "mla" else LANES
    dv = MLA_V if mode == "mla" else LANES
    nh = q.shape[-1] // qw
    in_specs = [pl.BlockSpec((1, tq, qw), lambda b, h, i: (b, i, h)),
                pl.BlockSpec((1, T, qw), lambda b, h, i: (b, 0, h)),
                pl.BlockSpec((1, LANES, T), lambda b, h, i: (b, h, 0))]
    for a in extra:
        in_specs.append(pl.BlockSpec(a.shape, lambda b, h, i, nd=a.ndim: (0,) * nd))
    return pl.pallas_call(
        functools.partial(_attn_body, mode, n_lat, n_ctx, tk, lam_init),
        grid=(B, nh, T // tq),
        in_specs=in_specs,
        out_specs=pl.BlockSpec((1, tq, LANES), lambda b, h, i: (b, i, h)),
        out_shape=jax.ShapeDtypeStruct((B, T, nh * LANES), BF16),
        scratch_shapes=[pltpu.VMEM((2, 2, tk, tq), F32), pltpu.VMEM((2, 1, tq), F32),
                        pltpu.VMEM((2, 1, tq), F32), pltpu.VMEM((2, dv, tq), F32)],
        compiler_params=_cp("parallel", "parallel", "arbitrary"),
        name="attn_" + mode,
    )(q, k, vt, *extra)


def _cumsum_rows(tri, x):
    hi = x.astype(BF16)
    r1 = x - hi.astype(F32)
    mid = r1.astype(BF16)
    lo = (r1 - mid.astype(F32)).astype(BF16)
    return _mm(tri, hi) + _mm(tri, mid) + _mm(tri, lo)


def _gla_scan_body(n_chunks, *refs):
    (qf_ref, kf_ref, vf_ref, vtf_ref, af_ref, qb_ref, kb_ref, vb_ref, vtb_ref, ab_ref,
     of_ref, ob_ref, sf_sc, sb_sc) = refs
    C = GLA_CHUNK

    @pl.when(pl.program_id(2) == 0)
    def _():
        sf_sc[...] = jnp.zeros_like(sf_sc)
        sb_sc[...] = jnp.zeros_like(sb_sc)

    r = lax.broadcasted_iota(jnp.int32, (C, C), 0)
    cidx = lax.broadcasted_iota(jnp.int32, (C, C), 1)
    lower = cidx <= r

    def chunk(ci, q_ref, k_ref, v_ref, vt_ref, a_ref, o_ref, s_sc, reverse):
        rows = slice(ci * C, (ci + 1) * C)
        keep = jnp.logical_not(lower) | (cidx == r) if reverse else lower
        tri = jnp.where(keep, 1.0, 0.0).astype(BF16)
        cum = _cumsum_rows(tri, a_ref[0, rows, :])
        total = cum[0:1, :] if reverse else cum[C - 1:C, :]
        q = q_ref[0, rows, :]
        k = k_ref[0, rows, :]
        qd = (q * jnp.exp(cum)).astype(BF16)
        ki = (k * jnp.exp(-cum)).astype(BF16)
        kd = (k * jnp.exp(total - cum)).astype(BF16)
        att = jnp.where(keep, _nt(qd, ki), 0.0).astype(BF16)
        st = s_sc[...]
        o_ref[0, rows, :] = _nt(qd, st.astype(BF16)) + _mm(att, v_ref[0, rows, :])
        s_sc[...] = st * jnp.exp(total) + _mm(vt_ref[0, :, rows], kd)

    for ci in range(n_chunks):
        chunk(ci, qf_ref, kf_ref, vf_ref, vtf_ref, af_ref, of_ref, sf_sc, False)
        chunk(n_chunks - 1 - ci, qb_ref, kb_ref, vb_ref, vtb_ref, ab_ref, ob_ref, sb_sc, True)


def _gla_scan(q, k, v, vt, af, ab, n_lat):
    B, T, dk = q.shape
    dm = v.shape[-1]
    dkh, dvh = dk // GLA_HEADS, dm // GLA_HEADS
    lb = 2 * GLA_CHUNK
    nb = T // lb
    n_lat_b = n_lat // lb
    n_ctx_b = nb - n_lat_b

    def fwd(j):
        return jnp.where(j < n_ctx_b, n_lat_b + j, j - n_ctx_b)

    def bwd(j):
        return jnp.where(j < n_ctx_b, nb - 1 - j, nb - 1 - j)

    def specs(blk):
        return [pl.BlockSpec((1, lb, dkh), lambda b, h, j: (b, blk(j), h)),
                pl.BlockSpec((1, lb, dkh), lambda b, h, j: (b, blk(j), h)),
                pl.BlockSpec((1, lb, dvh), lambda b, h, j: (b, blk(j), h)),
                pl.BlockSpec((1, dvh, lb), lambda b, h, j: (b, h, blk(j))),
                pl.BlockSpec((1, lb, dkh), lambda b, h, j: (b, blk(j), h))]

    o_sds = jax.ShapeDtypeStruct((B, T, dm), F32)
    return pl.pallas_call(
        functools.partial(_gla_scan_body, lb // GLA_CHUNK),
        grid=(B, GLA_HEADS, nb),
        in_specs=specs(fwd) + specs(bwd),
        out_specs=[pl.BlockSpec((1, lb, dvh), lambda b, h, j: (b, fwd(j), h)),
                   pl.BlockSpec((1, lb, dvh), lambda b, h, j: (b, bwd(j), h))],
        out_shape=[o_sds, o_sds],
        scratch_shapes=[pltpu.VMEM((dvh, dkh), F32), pltpu.VMEM((dvh, dkh), F32)],
        compiler_params=_cp("parallel", "parallel", "arbitrary"),
        name="gla_scan",
    )(q, k, v, vt, af, q, k, v, vt, ab)


def _moe_body(ne, h_ref, g_ref, tri_ref, wg_ref, wu_ref, wdt_ref, y_ref, key_sc, gt_sc, yt_sc):
    e = pl.program_id(1)
    tm = h_ref.shape[0]
    C = MOE_SLOTS

    @pl.when(e == 0)
    def _():
        gt = g_ref[...].T[:ne]
        sel = gt != 0.0
        rank = _mm(jnp.where(sel, 1.0, 0.0).astype(BF16), tri_ref[...])
        key_sc[...] = jnp.where(sel, rank, -1.0)
        gt_sc[...] = gt
        yt_sc[...] = jnp.zeros_like(yt_sc)

    keys = [key_sc[pl.ds(e * MOE_PAIR + k, 1), :] for k in range(MOE_PAIR)]
    gates = [gt_sc[pl.ds(e * MOE_PAIR + k, 1), :] for k in range(MOE_PAIR)]

    def select(k, r):
        slot = (lax.broadcasted_iota(jnp.int32, (C, tm), 0) + r * C).astype(F32)
        hit = keys[k] == slot
        return jnp.where(hit, 1.0, 0.0).astype(BF16), jnp.where(hit, gates[k], 0.0).astype(BF16)

    def expert(k, xg):
        a = _mm(xg, wg_ref[k])
        u = _mm(xg, wu_ref[k])
        act = (_silu(a) * u).astype(BF16)
        return _nt(wdt_ref[k], act).astype(BF16)

    def run(ks, r):
        sel = [select(k, r) for k in ks]
        xg = _mm(jnp.concatenate([p for p, _ in sel], axis=0), h_ref[...]).astype(BF16)
        zt = [expert(k, xg[i * C:(i + 1) * C]) for i, k in enumerate(ks)]
        yt_sc[...] += _mm(jnp.concatenate(zt, axis=1), jnp.concatenate([pg for _, pg in sel], axis=0))

    run(list(range(MOE_PAIR)), 0)
    for k in range(MOE_PAIR):
        n_chunks = (jnp.max(keys[k]).astype(jnp.int32) + C) // C

        def more(r, carry, k=k):
            run([k], r)
            return carry

        lax.fori_loop(1, n_chunks, more, 0)

    @pl.when(e == ne // MOE_PAIR - 1)
    def _():
        y_ref[...] = yt_sc[...].T


def _moe(h2, gates, wg, wu, wdt):
    n, d = h2.shape
    ne, _, f = wg.shape
    tm = math.gcd(n, MOE_TILE)
    tri = jnp.triu(jnp.ones((tm, tm), F32), 1).astype(BF16)
    return pl.pallas_call(
        functools.partial(_moe_body, ne),
        grid=(n // tm, ne // MOE_PAIR),
        in_specs=[pl.BlockSpec((tm, d), lambda t, e: (t, 0)),
                  pl.BlockSpec((tm, LANES), lambda t, e: (t, 0)),
                  pl.BlockSpec((tm, tm), lambda t, e: (0, 0)),
                  pl.BlockSpec((MOE_PAIR, d, f), lambda t, e: (e, 0, 0)),
                  pl.BlockSpec((MOE_PAIR, d, f), lambda t, e: (e, 0, 0)),
                  pl.BlockSpec((MOE_PAIR, d, f), lambda t, e: (e, 0, 0))],
        out_specs=pl.BlockSpec((tm, d), lambda t, e: (t, 0)),
        out_shape=jax.ShapeDtypeStruct((n, d), F32),
        scratch_shapes=[pltpu.VMEM((ne, tm), F32), pltpu.VMEM((ne, tm), F32), pltpu.VMEM((d, tm), F32)],
        compiler_params=_cp("parallel", "arbitrary"),
        name="moe",
    )(h2, gates, tri, wg, wu, wdt)


def _rope_angles(n_lat, n_ctx, rot_dim):
    rows = n_lat // GRID_W
    row = jnp.repeat(jnp.arange(rows, dtype=F32), GRID_W)
    col = jnp.tile(jnp.arange(GRID_W, dtype=F32), rows)
    n_freq = rot_dim // 4
    inv_freq = ROPE_BASE ** (-jnp.arange(n_freq, dtype=F32) / n_freq)
    ang = jnp.concatenate([row[:, None] * inv_freq, col[:, None] * inv_freq], axis=-1)
    cos = jnp.concatenate([jnp.cos(ang), jnp.ones((n_ctx, rot_dim // 2), F32)], axis=0)
    sin = jnp.concatenate([jnp.sin(ang), jnp.zeros((n_ctx, rot_dim // 2), F32)], axis=0)
    return cos, sin


def _swap_halves(w, width):
    lead = w.shape[:-1]
    w = w.reshape(*lead, -1, 2, width // 2)
    return jnp.flip(w, axis=-2).reshape(*lead, -1)


def kernel(x, c, ctx, c_ctx, w_ada, b_ada, norm_mix, norm_ffn, mla_w_dqkv, mla_q_norm, mla_w_uq, mla_kv_norm, mla_w_ukv, mla_w_o, diff_w_qkv, diff_lambda, diff_subln, diff_w_o, gla_w_in, gla_w_gate_up, gla_b_gate, gla_head_norm, gla_w_o, moe_w_group, moe_b_group, moe_w_router, moe_b_router, moe_w_gate, moe_w_up, moe_w_down, final_norm):
    B, n_lat, D = x.shape
    n_ctx = ctx.shape[1]
    T = n_lat + n_ctx
    depth = w_ada.shape[0]
    assert n_lat % ROW_TILE == 0 and n_ctx % ROW_TILE == 0 and B + 1 <= 8 and D % LANES == 0
    R = _Rows(B, T, n_lat)

    cvec = jnp.zeros((8, D), F32).at[:B].set(c).at[B].set(c_ctx)
    mods = _ada_all(cvec, w_ada, b_ada).reshape(depth, 8, N_MOD, D)

    stream = jnp.concatenate([x, ctx], axis=1)
    y_prev = None

    cos_a, sin_a = _rope_angles(n_lat, n_ctx, MLA_ROPE)
    ones, zeros = jnp.ones((T, MLA_NOPE), F32), jnp.zeros((T, LANES - MLA_NOPE - MLA_ROPE), F32)
    mla_cos = jnp.concatenate([ones, cos_a, cos_a, zeros], axis=1)
    mla_sin = jnp.concatenate([0 * ones, -sin_a, sin_a, zeros], axis=1)
    cos_b, sin_b = _rope_angles(n_lat, n_ctx, DIFF_HEAD_DIM)
    diff_cos = jnp.concatenate([cos_b] * 4, axis=1)
    diff_sin = jnp.concatenate([-sin_b, sin_b] * 2, axis=1)
    diff_scale = DIFF_HEAD_DIM ** -0.5 * math.log2(math.e)
    mla_qscale = (MLA_NOPE + MLA_ROPE) ** -0.5 * math.log2(math.e)

    for li in range(depth):
        last = li == depth - 1
        kind, j = li % 3, li // 3
        mod = mods[li]
        has_prev = y_prev is not None
        head = [stream] + ([y_prev, mods[li - 1]] if has_prev else []) + [mod, norm_mix[li].reshape(1, D)]
        head_specs = [R.rows(D)] + ([R.rows(D), R.mod(D)] if has_prev else []) + [R.mod(D), R.full(head[-1])]
        x1_out = ([R.out(D, F32)], [R.rows(D)]) if has_prev else ([], [])

        if kind == 0:
            wd = mla_w_dqkv[j]
            pe_w = wd[:, MLA_Q_RANK + MLA_KV_RANK:]
            padl, padr = jnp.zeros((D, MLA_NOPE), F32), jnp.zeros((D, LANES - MLA_NOPE - MLA_ROPE), F32)
            wdown = jnp.concatenate([wd[:, :MLA_Q_RANK + MLA_KV_RANK], padl, pe_w, padr,
                                     padl, _swap_halves(pe_w, MLA_ROPE), padr], axis=1).astype(BF16)
            wq = mla_w_uq[j].reshape(MLA_Q_RANK, MLA_HEADS, MLA_NOPE + MLA_ROPE)
            pad = jnp.zeros((MLA_Q_RANK, MLA_HEADS, LANES - MLA_NOPE - MLA_ROPE), F32)
            wq_a = jnp.concatenate([wq, pad], axis=-1).reshape(MLA_Q_RANK, -1).astype(BF16)
            wq_s = jnp.concatenate([0 * wq[..., :MLA_NOPE], _swap_halves(wq[..., MLA_NOPE:], MLA_ROPE), pad],
                                   axis=-1).reshape(MLA_Q_RANK, -1).astype(BF16)
            wkv = mla_w_ukv[j].reshape(MLA_KV_RANK, MLA_HEADS, MLA_NOPE + MLA_V)
            wk = jnp.concatenate([wkv[..., :MLA_NOPE], jnp.zeros((MLA_KV_RANK, MLA_HEADS, LANES - MLA_NOPE), F32)],
                                 axis=-1).reshape(MLA_KV_RANK, -1).astype(BF16)
            wvt = wkv[..., MLA_NOPE:].reshape(MLA_KV_RANK, -1).T.astype(BF16)
            ins = head + [wdown, mla_q_norm[j].reshape(1, -1), mla_kv_norm[j].reshape(1, -1),
                          wq_a, wq_s, wk, wvt, mla_cos * mla_qscale, mla_sin * mla_qscale, mla_cos, mla_sin]
            specs = head_specs + [R.full(a) for a in ins[len(head):-4]] + [R.table(LANES)] * 4
            outs = pl.pallas_call(
                functools.partial(_mla_proj_body, has_prev), grid=R.grid, in_specs=specs,
                out_specs=x1_out[1] + [R.rows(MLA_HEADS * LANES), R.rows(MLA_HEADS * LANES), R.cols(MLA_HEADS * MLA_V)],
                out_shape=x1_out[0] + [R.out(MLA_HEADS * LANES, BF16), R.out(MLA_HEADS * LANES, BF16),
                                       jax.ShapeDtypeStruct((B, MLA_HEADS * MLA_V, T), BF16)],
                compiler_params=_cp("parallel", "parallel"), name="mla_proj")(*ins)
            if has_prev:
                stream = outs[0]
            q, k, vt = outs[-3:]
            mix = [_attention("mla", q, k, vt, n_lat, n_ctx)]
            mix_specs = [R.rows(D)]
            w_o = mla_w_o[j]
        elif kind == 1:
            lam_init = 0.8 - 0.6 * math.exp(-0.3 * li)
            wq, wk, wv = jnp.split(diff_w_qkv[j], 3, axis=1)
            w = jnp.concatenate([wq, wk, _swap_halves(wq, DIFF_HEAD_DIM), _swap_halves(wk, DIFF_HEAD_DIM)],
                                axis=1).astype(BF16)
            wvt = wv.T.astype(BF16)
            ins = head + [w, wvt, diff_cos * diff_scale, diff_sin * diff_scale, diff_cos, diff_sin]
            specs = head_specs + [R.full(w), R.full(wvt)] + [R.table(LANES)] * 4
            outs = pl.pallas_call(
                functools.partial(_diff_proj_body, has_prev), grid=R.grid, in_specs=specs,
                out_specs=x1_out[1] + [R.rows(D), R.rows(D), R.cols(D)],
                out_shape=x1_out[0] + [R.out(D, BF16), R.out(D, BF16), jax.ShapeDtypeStruct((B, D, T), BF16)],
                compiler_params=_cp("parallel", "parallel"), name="diff_proj")(*ins)
            if has_prev:
                stream = outs[0]
            q, k, vt = outs[-3:]
            mix = [_attention("diff", q, k, vt, n_lat, n_ctx,
                              extra=(diff_lambda[j], diff_subln[j].reshape(-1, 1)), lam_init=lam_init)]
            mix_specs = [R.rows(D)]
            w_o = diff_w_o[j]
        else:
            dk = D // 2
            win = gla_w_in[j]
            zpad = jnp.zeros((D, LANES - 2 * GLA_GATE_RANK), F32)
            w = jnp.concatenate([win, zpad], axis=1).astype(BF16)
            wvt = win[:, 2 * dk:2 * dk + D].T.astype(BF16)
            wgu = jnp.zeros((LANES, 2 * dk), F32)
            wgu = wgu.at[:GLA_GATE_RANK, :dk].set(gla_w_gate_up[j, 0])
            wgu = wgu.at[GLA_GATE_RANK:2 * GLA_GATE_RANK, dk:].set(gla_w_gate_up[j, 1]).astype(BF16)
            bg = gla_b_gate[j].reshape(1, 2 * dk)
            ins = head + [w, wvt, wgu, bg]
            specs = head_specs + [R.full(a) for a in ins[len(head):]]
            vt_spec = pl.BlockSpec((1, D, R.tm), lambda b, t: (b, 0, t))
            outs = pl.pallas_call(
                functools.partial(_gla_proj_body, has_prev), grid=R.grid, in_specs=specs,
                out_specs=x1_out[1] + [R.rows(dk), R.rows(dk), R.rows(D), vt_spec, R.rows(D), R.rows(dk), R.rows(dk)],
                out_shape=x1_out[0] + [R.out(dk, F32), R.out(dk, F32), R.out(D, BF16),
                                       jax.ShapeDtypeStruct((B, D, T), BF16), R.out(D, F32),
                                       R.out(dk, F32), R.out(dk, F32)],
                compiler_params=_cp("parallel", "parallel"), name="gla_proj")(*ins)
            if has_prev:
                stream = outs[0]
            q, k, v, vt, g, af, ab = outs[-7:]
            o_f, o_b = _gla_scan(q, k, v, vt, af, ab, n_lat)
            hn = gla_head_norm[j].reshape(1, -1)
            mix = [o_f, o_b, g, hn]
            mix_specs = [R.rows(D), R.rows(D), R.rows(D), R.full(hn)]
            w_o = gla_w_o[j]

        wr = jnp.concatenate([moe_w_group[li], jnp.moveaxis(moe_w_router[li], 0, 1).reshape(D, -1)], axis=1)
        n_r = wr.shape[1]
        wr = jnp.concatenate([wr, jnp.zeros((D, LANES - n_r), F32)], axis=1).astype(BF16)
        br = jnp.concatenate([moe_b_group[li], moe_b_router[li].reshape(-1), jnp.zeros((LANES - n_r,), F32)]).reshape(1, LANES)
        tail = [mod, norm_ffn[li].reshape(1, D), w_o.astype(BF16), wr, br]
        stream, h2, gates = pl.pallas_call(
            functools.partial(_post_body, "gla" if kind == 2 else "attn"), grid=R.grid,
            in_specs=[R.rows(D)] + mix_specs + [R.mod(D)] + [R.full(a) for a in tail[1:]],
            out_specs=[R.rows(D), R.rows(D), R.rows(LANES)],
            out_shape=[R.out(D, F32), R.out(D, BF16), R.out(LANES, F32)],
            compiler_params=_cp("parallel", "parallel"), name="post_mixer")(stream, *mix, *tail)

        ne = MOE_GROUPS * MOE_EXPERTS
        f = moe_w_gate.shape[-1]
        y = _moe(h2.reshape(B * T, D), gates.reshape(B * T, LANES),
                 moe_w_gate[li].reshape(ne, D, f).astype(BF16), moe_w_up[li].reshape(ne, D, f).astype(BF16),
                 jnp.swapaxes(moe_w_down[li].reshape(ne, f, D), 1, 2).astype(BF16))
        y_prev = y.reshape(B, T, D)

    RL = _Rows(B, n_lat, n_lat)
    fin = final_norm.reshape(1, D)
    return pl.pallas_call(
        _final_body, grid=RL.grid,
        in_specs=[RL.rows(D), RL.rows(D), RL.mod(D), RL.full(fin)],
        out_specs=RL.rows(D), out_shape=RL.out(D, F32),
        compiler_params=_cp("parallel", "parallel"), name="final_norm")(stream, y_prev, mods[depth - 1], fin)
```

```python
import functools
import math

import jax
import jax.numpy as jnp
from jax import lax
from jax.experimental import pallas as pl
from jax.experimental.pallas import tpu as pltpu

F32 = jnp.float32
BF16 = jnp.bfloat16

EPS = 1e-6
ROPE_BASE = 10000.0
GRID_W = 64
N_MOD = 6
LANES = 128

MLA_HEADS = 16
MLA_Q_RANK = 256
MLA_KV_RANK = 256
MLA_NOPE = 64
MLA_ROPE = 32
MLA_V = 64
DIFF_HEAD_DIM = 64
GLA_HEADS = 4
GLA_GATE_RANK = 16
GLA_GATE_NORM = 16.0
GLA_CHUNK = 128
MOE_GROUPS = 4
MOE_EXPERTS = 4

ROW_TILE = 256
MOE_TILE = 1024
MOE_SLOTS = 256
MOE_PAIR = 2
ATTN_KEYS = 1024
VMEM_LIMIT = 52 * 1024 * 1024


def _cp(*sem):
    return pltpu.CompilerParams(dimension_semantics=sem, vmem_limit_bytes=VMEM_LIMIT)


def _mm(a, b):
    return jnp.dot(a, b, preferred_element_type=F32)


def _nt(a, b):
    return lax.dot_general(a, b, (((1,), (1,)), ((), ())), preferred_element_type=F32)


def _rms(x):
    return x * lax.rsqrt(jnp.mean(x * x, axis=-1, keepdims=True) + EPS)


def _silu(x):
    return x * (1.0 / (1.0 + jnp.exp(-x)))


def _ada_body(c_ref, w_ref, b_ref, o_ref):
    c = c_ref[...]
    o_ref[0] = _mm(_silu(c).astype(BF16), w_ref[0].astype(BF16)) + b_ref[0]


def _ada_all(cvec, w_ada, b_ada):
    depth, d, n6 = w_ada.shape
    tn = 1536
    return pl.pallas_call(
        _ada_body,
        grid=(depth, n6 // tn),
        in_specs=[pl.BlockSpec((8, d), lambda l, j: (0, 0)),
                  pl.BlockSpec((1, d, tn), lambda l, j: (l, 0, j)),
                  pl.BlockSpec((1, 1, tn), lambda l, j: (l, 0, j))],
        out_specs=pl.BlockSpec((1, 8, tn), lambda l, j: (l, 0, j)),
        out_shape=jax.ShapeDtypeStruct((depth, 8, n6), F32),
        compiler_params=_cp("parallel", "parallel"),
        name="adaln",
    )(cvec, w_ada, b_ada.reshape(depth, 1, n6))


class _Rows:
    def __init__(self, B, T, n_lat, tm=ROW_TILE):
        self.B, self.T, self.tm = B, T, tm
        self.n_lat_tiles = n_lat // tm
        self.grid = (B, T // tm)

    def rows(self, c):
        return pl.BlockSpec((1, self.tm, c), lambda b, t: (b, t, 0))

    def mod(self, d):
        nl, B = self.n_lat_tiles, self.B
        return pl.BlockSpec((1, N_MOD, d), lambda b, t: (jnp.where(t < nl, b, B), 0, 0))

    def cols(self, c):
        return pl.BlockSpec((1, c, self.tm), lambda b, t: (b, 0, t))

    def table(self, c):
        return pl.BlockSpec((self.tm, c), lambda b, t: (t, 0))

    @staticmethod
    def full(a):
        nd = a.ndim
        return pl.BlockSpec(a.shape, lambda b, t: (0,) * nd)

    def out(self, c, dtype):
        return jax.ShapeDtypeStruct((self.B, self.T, c), dtype)


def _prologue(refs, has_prev, k_shift):
    if has_prev:
        x_ref, y_ref, modp_ref, mod_ref, gain_ref = refs[:5]
        x = x_ref[0] + modp_ref[0, 5:6, :] * y_ref[0]
        rest = refs[5:]
    else:
        x_ref, mod_ref, gain_ref = refs[:3]
        x = x_ref[0]
        rest = refs[3:]
    h = _rms(x) * gain_ref[...]
    h = h * (1.0 + mod_ref[0, k_shift + 1:k_shift + 2, :]) + mod_ref[0, k_shift:k_shift + 1, :]
    return x, h, rest


def _mla_proj_body(has_prev, *refs):
    x, h, rest = _prologue(refs, has_prev, 0)
    (wd_ref, qn_ref, kvn_ref, wq_ref, wqs_ref, wk_ref, wvt_ref, cq_ref, sq_ref, c_ref, s_ref), outs = rest[:11], rest[11:]
    if has_prev:
        x1_ref, q_ref, k_ref, vt_ref = outs
        x1_ref[0] = x
    else:
        q_ref, k_ref, vt_ref = outs
    d = _mm(h.astype(BF16), wd_ref[...])
    cos, sin = c_ref[...], s_ref[...]
    cos_q, sin_q = cq_ref[...], sq_ref[...]
    cq = (_rms(d[:, :MLA_Q_RANK]) * qn_ref[...]).astype(BF16)
    ckv = (_rms(d[:, MLA_Q_RANK:MLA_Q_RANK + MLA_KV_RANK]) * kvn_ref[...]).astype(BF16)
    o = MLA_Q_RANK + MLA_KV_RANK
    pe = d[:, o:o + LANES] * cos + d[:, o + LANES:o + 2 * LANES] * sin
    qa = _mm(cq, wq_ref[...])
    qb = _mm(cq, wqs_ref[...])
    kn = _mm(ckv, wk_ref[...])
    for hd in range(MLA_HEADS):
        sl = slice(hd * LANES, (hd + 1) * LANES)
        q_ref[0, :, sl] = (qa[:, sl] * cos_q + qb[:, sl] * sin_q).astype(BF16)
        k_ref[0, :, sl] = (kn[:, sl] + pe).astype(BF16)
    vt_ref[0] = _nt(wvt_ref[...], ckv).astype(BF16)


def _diff_proj_body(has_prev, *refs):
    x, h, rest = _prologue(refs, has_prev, 0)
    (w_ref, wvt_ref, cq_ref, sq_ref, ck_ref, sk_ref), outs = rest[:6], rest[6:]
    if has_prev:
        x1_ref, q_ref, k_ref, vt_ref = outs
        x1_ref[0] = x
    else:
        q_ref, k_ref, vt_ref = outs
    dm = h.shape[-1]
    hb = h.astype(BF16)
    p = _mm(hb, w_ref[...])
    cq, sq, ck, sk = cq_ref[...], sq_ref[...], ck_ref[...], sk_ref[...]
    for hd in range(dm // LANES):
        sl = slice(hd * LANES, (hd + 1) * LANES)
        q_ref[0, :, sl] = (p[:, sl] * cq + p[:, 2 * dm + hd * LANES:2 * dm + (hd + 1) * LANES] * sq).astype(BF16)
        k_ref[0, :, sl] = (p[:, dm + hd * LANES:dm + (hd + 1) * LANES] * ck
                           + p[:, 3 * dm + hd * LANES:3 * dm + (hd + 1) * LANES] * sk).astype(BF16)
    vt_ref[0] = _nt(wvt_ref[...], hb).astype(BF16)


def _gla_proj_body(has_prev, *refs):
    x, h, rest = _prologue(refs, has_prev, 0)
    (w_ref, wvt_ref, wgu_ref, bg_ref), outs = rest[:4], rest[4:]
    if has_prev:
        x1_ref, q_ref, k_ref, v_ref, vt_ref, g_ref, af_ref, ab_ref = outs
        x1_ref[0] = x
    else:
        q_ref, k_ref, v_ref, vt_ref, g_ref, af_ref, ab_ref = outs
    dm = h.shape[-1]
    dk = dm // 2
    hb = h.astype(BF16)
    p = _mm(hb, w_ref[...])
    q_ref[0] = p[:, :dk] * ((dk // GLA_HEADS) ** -0.5)
    k_ref[0] = p[:, dk:2 * dk]
    v_ref[0] = p[:, 2 * dk:2 * dk + dm].astype(BF16)
    vt_ref[0] = _nt(wvt_ref[...], hb).astype(BF16)
    g_ref[0] = p[:, 2 * dk + dm:2 * dk + 2 * dm]
    z = p[:, 2 * dk + 2 * dm:].astype(BF16)
    a = _mm(z, wgu_ref[...]) + bg_ref[...]
    ls = (jnp.minimum(a, 0.0) - jnp.log(1.0 + jnp.exp(-jnp.abs(a)))) / GLA_GATE_NORM
    af_ref[0] = ls[:, :dk]
    ab_ref[0] = ls[:, dk:]


def _route(logits_t):
    def col(i):
        return logits_t[i:i + 1, :]

    lg = [col(i) for i in range(MOE_GROUPS)]
    gmax = functools.reduce(jnp.maximum, lg)
    onehot, taken = [], None
    for i in range(MOE_GROUPS):
        hit = lg[i] == gmax
        if taken is not None:
            hit = jnp.logical_and(hit, jnp.logical_not(taken))
        taken = hit if taken is None else jnp.logical_or(taken, hit)
        onehot.append(hit)
    grp_w = 1.0 / functools.reduce(lambda a, b: a + b, [jnp.exp(l - gmax) for l in lg])
    le = []
    for e in range(MOE_EXPERTS):
        v = None
        for g in range(MOE_GROUPS):
            c = jnp.where(onehot[g], col(MOE_GROUPS + g * MOE_EXPERTS + e), 0.0)
            v = c if v is None else v + c
        le.append(v)
    vmax = functools.reduce(jnp.maximum, le)
    num = []
    for e in range(MOE_EXPERTS):
        rank = None
        for o in range(MOE_EXPERTS):
            if o == e:
                continue
            ahead = (le[o] >= le[e]) if o < e else (le[o] > le[e])
            r = jnp.where(ahead, 1.0, 0.0)
            rank = r if rank is None else rank + r
        num.append(jnp.where(rank < 2.0, jnp.exp(le[e] - vmax), 0.0))
    scale = grp_w / functools.reduce(lambda a, b: a + b, num)
    return [jnp.where(onehot[g], num[e] * scale, 0.0) for g in range(MOE_GROUPS) for e in range(MOE_EXPERTS)]


def _post_body(kind, *refs):
    x1_ref = refs[0]
    if kind == "gla":
        of_ref, ob_ref, g_ref, hn_ref = refs[1:5]
        rest = refs[5:]
        o = of_ref[0] + ob_ref[0]
        hn = hn_ref[...]
        dv = hn.shape[-1]
        parts = [_rms(o[:, i * dv:(i + 1) * dv]) * hn for i in range(o.shape[-1] // dv)]
        a = (jnp.concatenate(parts, axis=-1) * _silu(g_ref[0])).astype(BF16)
    else:
        a = refs[1][0]
        rest = refs[2:]
    mod_ref, gain_ref, wo_ref, wr_ref, br_ref, x2_ref, h2_ref, gate_ref = rest
    x2 = x1_ref[0] + mod_ref[0, 2:3, :] * _mm(a, wo_ref[...])
    x2_ref[0] = x2
    h2 = _rms(x2) * gain_ref[...]
    h2 = (h2 * (1.0 + mod_ref[0, 4:5, :]) + mod_ref[0, 3:4, :]).astype(BF16)
    h2_ref[0] = h2
    for i, row in enumerate(_route((_mm(h2, wr_ref[...]) + br_ref[...]).T)):
        gate_ref[i:i + 1, :] = row


def _final_body(x_ref, y_ref, mod_ref, gain_ref, o_ref):
    x = x_ref[0] + mod_ref[0, 5:6, :] * y_ref[0]
    o_ref[0] = _rms(x) * gain_ref[...]


def _attn_body(mode, n_lat, n_ctx, tk, lam_init, *refs):
    if mode == "mla":
        q_ref, k_ref, vt_ref, o_ref, s_sc, m_sc, l_sc, acc_sc = refs
    else:
        q_ref, k_ref, vt_ref, lam_ref, sub_ref, o_ref, s_sc, m_sc, l_sc, acc_sc = refs
    tq = q_ref.shape[1]
    dv = acc_sc.shape[1]
    n_steps = (n_lat + n_ctx) // tk
    is_lat = pl.program_id(2) < (n_lat // tq)

    def q_of(i):
        if mode == "mla":
            return q_ref[0, :, i * LANES:(i + 1) * LANES]
        q = q_ref[0]
        lane = lax.broadcasted_iota(jnp.int32, q.shape, 1)
        keep = (lane < DIFF_HEAD_DIM) if i == 0 else (lane >= DIFF_HEAD_DIM)
        return jnp.where(keep, q, jnp.zeros_like(q))

    qs = [q_of(0), q_of(1)]

    def scores(i, rows):
        k = k_ref[0, rows, i * LANES:(i + 1) * LANES] if mode == "mla" else k_ref[0, rows, :]
        return _nt(k, qs[i])

    def vt_of(i, cols):
        return vt_ref[0, i * dv:(i + 1) * dv, cols] if mode == "mla" else vt_ref[0, :, cols]

    def absorb(i, s, cols, first=False):
        m_new = jnp.max(s, axis=0, keepdims=True)
        if not first:
            m_old = m_sc[i]
            m_new = jnp.maximum(m_old, m_new)
            alpha = jnp.exp2(m_old - m_new)
        p = jnp.exp2(s - m_new)
        l = jnp.sum(p, axis=0, keepdims=True)
        pv = _mm(vt_of(i, cols), p.astype(BF16))
        l_sc[i] = l if first else alpha * l_sc[i] + l
        acc_sc[i] = pv if first else alpha * acc_sc[i] + pv
        m_sc[i] = m_new

    @pl.when(jnp.logical_not(is_lat))
    def _():
        ctx = slice(n_lat, n_lat + n_ctx)
        for i in range(2):
            absorb(i, scores(i, ctx), ctx, first=True)

    @pl.when(is_lat)
    def _():
        for i in range(2):
            s_sc[0, i] = scores(i, slice(0, tk))
        for c in range(n_steps):
            slot = c % 2
            for i in range(2):
                if c + 1 < n_steps:
                    s_next = scores(i, slice((c + 1) * tk, (c + 2) * tk))
                absorb(i, s_sc[slot, i], slice(c * tk, (c + 1) * tk), first=(c == 0))
                if c + 1 < n_steps:
                    s_sc[1 - slot, i] = s_next

    o0 = acc_sc[0] * (1.0 / l_sc[0])
    o1 = acc_sc[1] * (1.0 / l_sc[1])
    if mode == "mla":
        o = jnp.concatenate([o0, o1], axis=0)
    else:
        lam = lam_ref[...]
        lam_full = (jnp.exp(jnp.sum(lam[0:1] * lam[1:2], axis=-1, keepdims=True))
                    - jnp.exp(jnp.sum(lam[2:3] * lam[3:4], axis=-1, keepdims=True)) + lam_init)
        o = o0 - lam_full * o1
        o = o * lax.rsqrt(jnp.mean(o * o, axis=0, keepdims=True) + EPS) * sub_ref[...] * (1.0 - lam_init)
    o_ref[0] = o.T.astype(BF16)


def _attention(mode, q, k, vt, n_lat, n_ctx, extra=(), lam_init=0.0):
    B, T, _ = q.shape
    tq = ROW_TILE
    tk = max(t for t in range(256, ATTN_KEYS + 1, 256) if T % t == 0)
    qw = 2 * LANES if mode == "mla" else LANES
    dv = MLA_V if mode == "mla" else LANES
    nh = q.shape[-1] // qw
    in_specs = [pl.BlockSpec((1, tq, qw), lambda b, h, i: (b, i, h)),
                pl.BlockSpec((1, T, qw), lambda b, h, i: (b, 0, h)),
                pl.BlockSpec((1, LANES, T), lambda b, h, i: (b, h, 0))]
    for a in extra:
        in_specs.append(pl.BlockSpec(a.shape, lambda b, h, i, nd=a.ndim: (0,) * nd))
    return pl.pallas_call(
        functools.partial(_attn_body, mode, n_lat, n_ctx, tk, lam_init),
        grid=(B, nh, T // tq),
        in_specs=in_specs,
        out_specs=pl.BlockSpec((1, tq, LANES), lambda b, h, i: (b, i, h)),
        out_shape=jax.ShapeDtypeStruct((B, T, nh * LANES), BF16),
        scratch_shapes=[pltpu.VMEM((2, 2, tk, tq), F32), pltpu.VMEM((2, 1, tq), F32),
                        pltpu.VMEM((2, 1, tq), F32), pltpu.VMEM((2, dv, tq), F32)],
        compiler_params=_cp("parallel", "parallel", "arbitrary"),
        name="attn_" + mode,
    )(q, k, vt, *extra)


def _cumsum_rows(tri, x):
    hi = x.astype(BF16)
    r1 = x - hi.astype(F32)
    mid = r1.astype(BF16)
    lo = (r1 - mid.astype(F32)).astype(BF16)
    return _mm(tri, hi) + _mm(tri, mid) + _mm(tri, lo)


def _gla_scan_body(n_chunks, *refs):
    (qf_ref, kf_ref, vf_ref, vtf_ref, af_ref, qb_ref, kb_ref, vb_ref, vtb_ref, ab_ref,
     of_ref, ob_ref, sf_sc, sb_sc) = refs
    C = GLA_CHUNK
    n_heads, dvh, dkh = sf_sc.shape

    @pl.when(pl.program_id(1) == 0)
    def _():
        sf_sc[...] = jnp.zeros_like(sf_sc)
        sb_sc[...] = jnp.zeros_like(sb_sc)

    r = lax.broadcasted_iota(jnp.int32, (C, C), 0)
    cidx = lax.broadcasted_iota(jnp.int32, (C, C), 1)
    lower = cidx <= r

    def chunk(ci, hd, q_ref, k_ref, v_ref, vt_ref, a_ref, o_ref, s_sc, reverse):
        rows = slice(ci * C, (ci + 1) * C)
        kl = slice(hd * dkh, (hd + 1) * dkh)
        vl = slice(hd * dvh, (hd + 1) * dvh)
        keep = jnp.logical_not(lower) | (cidx == r) if reverse else lower
        tri = jnp.where(keep, 1.0, 0.0).astype(BF16)
        cum = _cumsum_rows(tri, a_ref[0, rows, kl])
        total = cum[0:1, :] if reverse else cum[C - 1:C, :]
        q = q_ref[0, rows, kl]
        k = k_ref[0, rows, kl]
        qd = (q * jnp.exp(cum)).astype(BF16)
        ki = (k * jnp.exp(-cum)).astype(BF16)
        kd = (k * jnp.exp(total - cum)).astype(BF16)
        att = jnp.where(keep, _nt(qd, ki), 0.0).astype(BF16)
        st = s_sc[hd]
        o_ref[0, rows, vl] = _nt(qd, st.astype(BF16)) + _mm(att, v_ref[0, rows, vl])
        s_sc[hd] = st * jnp.exp(total) + _mm(vt_ref[0, vl, rows], kd)

    for ci in range(n_chunks):
        for hd in range(n_heads):
            chunk(ci, hd, qf_ref, kf_ref, vf_ref, vtf_ref, af_ref, of_ref, sf_sc, False)
            chunk(n_chunks - 1 - ci, hd, qb_ref, kb_ref, vb_ref, vtb_ref, ab_ref, ob_ref, sb_sc, True)


def _gla_scan(q, k, v, vt, af, ab, n_lat):
    B, T, dk = q.shape
    dm = v.shape[-1]
    dkh, dvh = dk // GLA_HEADS, dm // GLA_HEADS
    lb = 2 * GLA_CHUNK
    nb = T // lb
    n_lat_b = n_lat // lb
    n_ctx_b = nb - n_lat_b

    def fwd(j):
        return jnp.where(j < n_ctx_b, n_lat_b + j, j - n_ctx_b)

    def bwd(j):
        return jnp.where(j < n_ctx_b, nb - 1 - j, nb - 1 - j)

    def specs(blk):
        return [pl.BlockSpec((1, lb, dk), lambda b, j: (b, blk(j), 0)),
                pl.BlockSpec((1, lb, dk), lambda b, j: (b, blk(j), 0)),
                pl.BlockSpec((1, lb, dm), lambda b, j: (b, blk(j), 0)),
                pl.BlockSpec((1, dm, lb), lambda b, j: (b, 0, blk(j))),
                pl.BlockSpec((1, lb, dk), lambda b, j: (b, blk(j), 0))]

    o_sds = jax.ShapeDtypeStruct((B, T, dm), F32)
    state = pltpu.VMEM((GLA_HEADS, dvh, dkh), F32)
    return pl.pallas_call(
        functools.partial(_gla_scan_body, lb // GLA_CHUNK),
        grid=(B, nb),
        in_specs=specs(fwd) + specs(bwd),
        out_specs=[pl.BlockSpec((1, lb, dm), lambda b, j: (b, fwd(j), 0)),
                   pl.BlockSpec((1, lb, dm), lambda b, j: (b, bwd(j), 0))],
        out_shape=[o_sds, o_sds],
        scratch_shapes=[state, state],
        compiler_params=_cp("parallel", "arbitrary"),
        name="gla_scan",
    )(q, k, v, vt, af, q, k, v, vt, ab)


def _moe_body(ne, h_ref, g_ref, tri_ref, wg_ref, wu_ref, wdt_ref, y_ref, key_sc, gt_sc, yt_sc):
    e = pl.program_id(1)
    tm = h_ref.shape[0]
    C = MOE_SLOTS

    @pl.when(e == 0)
    def _():
        gt = g_ref[...]
        sel = gt != 0.0
        rank = _mm(jnp.where(sel, 1.0, 0.0).astype(BF16), tri_ref[...])
        key_sc[...] = jnp.where(sel, rank, -1.0)
        gt_sc[...] = gt
        yt_sc[...] = jnp.zeros_like(yt_sc)

    keys = [key_sc[pl.ds(e * MOE_PAIR + k, 1), :] for k in range(MOE_PAIR)]
    gates = [gt_sc[pl.ds(e * MOE_PAIR + k, 1), :] for k in range(MOE_PAIR)]

    def select(k, r):
        slot = (lax.broadcasted_iota(jnp.int32, (C, tm), 0) + r * C).astype(F32)
        hit = keys[k] == slot
        return jnp.where(hit, 1.0, 0.0).astype(BF16), jnp.where(hit, gates[k], 0.0).astype(BF16)

    def expert(k, xg):
        a = _mm(xg, wg_ref[k])
        u = _mm(xg, wu_ref[k])
        act = (_silu(a) * u).astype(BF16)
        return _nt(wdt_ref[k], act).astype(BF16)

    def run(ks, r):
        sel = [select(k, r) for k in ks]
        xg = _mm(jnp.concatenate([p for p, _ in sel], axis=0), h_ref[...]).astype(BF16)
        zt = [expert(k, xg[i * C:(i + 1) * C]) for i, k in enumerate(ks)]
        yt_sc[...] += _mm(jnp.concatenate(zt, axis=1), jnp.concatenate([pg for _, pg in sel], axis=0))

    run(list(range(MOE_PAIR)), 0)
    for k in range(MOE_PAIR):
        n_chunks = (jnp.max(keys[k]).astype(jnp.int32) + C) // C

        def more(r, carry, k=k):
            run([k], r)
            return carry

        lax.fori_loop(1, n_chunks, more, 0)

    @pl.when(e == ne // MOE_PAIR - 1)
    def _():
        y_ref[...] = yt_sc[...].T


def _moe(h2, gates, wg, wu, wdt):
    n, d = h2.shape
    ne, _, f = wg.shape
    tm = math.gcd(n, MOE_TILE)
    tri = jnp.triu(jnp.ones((tm, tm), F32), 1).astype(BF16)
    return pl.pallas_call(
        functools.partial(_moe_body, ne),
        grid=(n // tm, ne // MOE_PAIR),
        in_specs=[pl.BlockSpec((tm, d), lambda t, e: (t, 0)),
                  pl.BlockSpec((ne, tm), lambda t, e: (0, t)),
                  pl.BlockSpec((tm, tm), lambda t, e: (0, 0)),
                  pl.BlockSpec((MOE_PAIR, d, f), lambda t, e: (e, 0, 0)),
                  pl.BlockSpec((MOE_PAIR, d, f), lambda t, e: (e, 0, 0)),
                  pl.BlockSpec((MOE_PAIR, d, f), lambda t, e: (e, 0, 0))],
        out_specs=pl.BlockSpec((tm, d), lambda t, e: (t, 0)),
        out_shape=jax.ShapeDtypeStruct((n, d), F32),
        scratch_shapes=[pltpu.VMEM((ne, tm), F32), pltpu.VMEM((ne, tm), F32), pltpu.VMEM((d, tm), F32)],
        compiler_params=_cp("parallel", "arbitrary"),
        name="moe",
    )(h2, gates, tri, wg, wu, wdt)


def _rope_angles(n_lat, n_ctx, rot_dim):
    rows = n_lat // GRID_W
    row = jnp.repeat(jnp.arange(rows, dtype=F32), GRID_W)
    col = jnp.tile(jnp.arange(GRID_W, dtype=F32), rows)
    n_freq = rot_dim // 4
    inv_freq = ROPE_BASE ** (-jnp.arange(n_freq, dtype=F32) / n_freq)
    ang = jnp.concatenate([row[:, None] * inv_freq, col[:, None] * inv_freq], axis=-1)
    cos = jnp.concatenate([jnp.cos(ang), jnp.ones((n_ctx, rot_dim // 2), F32)], axis=0)
    sin = jnp.concatenate([jnp.sin(ang), jnp.zeros((n_ctx, rot_dim // 2), F32)], axis=0)
    return cos, sin


def _swap_halves(w, width):
    lead = w.shape[:-1]
    w = w.reshape(*lead, -1, 2, width // 2)
    return jnp.flip(w, axis=-2).reshape(*lead, -1)


def kernel(x, c, ctx, c_ctx, w_ada, b_ada, norm_mix, norm_ffn, mla_w_dqkv, mla_q_norm, mla_w_uq, mla_kv_norm, mla_w_ukv, mla_w_o, diff_w_qkv, diff_lambda, diff_subln, diff_w_o, gla_w_in, gla_w_gate_up, gla_b_gate, gla_head_norm, gla_w_o, moe_w_group, moe_b_group, moe_w_router, moe_b_router, moe_w_gate, moe_w_up, moe_w_down, final_norm):
    B, n_lat, D = x.shape
    n_ctx = ctx.shape[1]
    T = n_lat + n_ctx
    depth = w_ada.shape[0]
    assert n_lat % ROW_TILE == 0 and n_ctx % ROW_TILE == 0 and B + 1 <= 8 and D % LANES == 0
    R = _Rows(B, T, n_lat)

    cvec = jnp.zeros((8, D), F32).at[:B].set(c).at[B].set(c_ctx)
    mods = _ada_all(cvec, w_ada, b_ada).reshape(depth, 8, N_MOD, D)

    stream = jnp.concatenate([x, ctx], axis=1)
    y_prev = None

    cos_a, sin_a = _rope_angles(n_lat, n_ctx, MLA_ROPE)
    ones, zeros = jnp.ones((T, MLA_NOPE), F32), jnp.zeros((T, LANES - MLA_NOPE - MLA_ROPE), F32)
    mla_cos = jnp.concatenate([ones, cos_a, cos_a, zeros], axis=1)
    mla_sin = jnp.concatenate([0 * ones, -sin_a, sin_a, zeros], axis=1)
    cos_b, sin_b = _rope_angles(n_lat, n_ctx, DIFF_HEAD_DIM)
    diff_cos = jnp.concatenate([cos_b] * 4, axis=1)
    diff_sin = jnp.concatenate([-sin_b, sin_b] * 2, axis=1)
    diff_scale = DIFF_HEAD_DIM ** -0.5 * math.log2(math.e)
    mla_qscale = (MLA_NOPE + MLA_ROPE) ** -0.5 * math.log2(math.e)

    for li in range(depth):
        last = li == depth - 1
        kind, j = li % 3, li // 3
        mod = mods[li]
        has_prev = y_prev is not None
        head = [stream] + ([y_prev, mods[li - 1]] if has_prev else []) + [mod, norm_mix[li].reshape(1, D)]
        head_specs = [R.rows(D)] + ([R.rows(D), R.mod(D)] if has_prev else []) + [R.mod(D), R.full(head[-1])]
        x1_out = ([R.out(D, F32)], [R.rows(D)]) if has_prev else ([], [])

        if kind == 0:
            wd = mla_w_dqkv[j]
            pe_w = wd[:, MLA_Q_RANK + MLA_KV_RANK:]
            padl, padr = jnp.zeros((D, MLA_NOPE), F32), jnp.zeros((D, LANES - MLA_NOPE - MLA_ROPE), F32)
            wdown = jnp.concatenate([wd[:, :MLA_Q_RANK + MLA_KV_RANK], padl, pe_w, padr,
                                     padl, _swap_halves(pe_w, MLA_ROPE), padr], axis=1).astype(BF16)
            wq = mla_w_uq[j].reshape(MLA_Q_RANK, MLA_HEADS, MLA_NOPE + MLA_ROPE)
            pad = jnp.zeros((MLA_Q_RANK, MLA_HEADS, LANES - MLA_NOPE - MLA_ROPE), F32)
            wq_a = jnp.concatenate([wq, pad], axis=-1).reshape(MLA_Q_RANK, -1).astype(BF16)
            wq_s = jnp.concatenate([0 * wq[..., :MLA_NOPE], _swap_halves(wq[..., MLA_NOPE:], MLA_ROPE), pad],
                                   axis=-1).reshape(MLA_Q_RANK, -1).astype(BF16)
            wkv = mla_w_ukv[j].reshape(MLA_KV_RANK, MLA_HEADS, MLA_NOPE + MLA_V)
            wk = jnp.concatenate([wkv[..., :MLA_NOPE], jnp.zeros((MLA_KV_RANK, MLA_HEADS, LANES - MLA_NOPE), F32)],
                                 axis=-1).reshape(MLA_KV_RANK, -1).astype(BF16)
            wvt = wkv[..., MLA_NOPE:].reshape(MLA_KV_RANK, -1).T.astype(BF16)
            ins = head + [wdown, mla_q_norm[j].reshape(1, -1), mla_kv_norm[j].reshape(1, -1),
                          wq_a, wq_s, wk, wvt, mla_cos * mla_qscale, mla_sin * mla_qscale, mla_cos, mla_sin]
            specs = head_specs + [R.full(a) for a in ins[len(head):-4]] + [R.table(LANES)] * 4
            outs = pl.pallas_call(
                functools.partial(_mla_proj_body, has_prev), grid=R.grid, in_specs=specs,
                out_specs=x1_out[1] + [R.rows(MLA_HEADS * LANES), R.rows(MLA_HEADS * LANES), R.cols(MLA_HEADS * MLA_V)],
                out_shape=x1_out[0] + [R.out(MLA_HEADS * LANES, BF16), R.out(MLA_HEADS * LANES, BF16),
                                       jax.ShapeDtypeStruct((B, MLA_HEADS * MLA_V, T), BF16)],
                compiler_params=_cp("parallel", "parallel"), name="mla_proj")(*ins)
            if has_prev:
                stream = outs[0]
            q, k, vt = outs[-3:]
            mix = [_attention("mla", q, k, vt, n_lat, n_ctx)]
            mix_specs = [R.rows(D)]
            w_o = mla_w_o[j]
        elif kind == 1:
            lam_init = 0.8 - 0.6 * math.exp(-0.3 * li)
            wq, wk, wv = jnp.split(diff_w_qkv[j], 3, axis=1)
            w = jnp.concatenate([wq, wk, _swap_halves(wq, DIFF_HEAD_DIM), _swap_halves(wk, DIFF_HEAD_DIM)],
                                axis=1).astype(BF16)
            wvt = wv.T.astype(BF16)
            ins = head + [w, wvt, diff_cos * diff_scale, diff_sin * diff_scale, diff_cos, diff_sin]
            specs = head_specs + [R.full(w), R.full(wvt)] + [R.table(LANES)] * 4
            outs = pl.pallas_call(
                functools.partial(_diff_proj_body, has_prev), grid=R.grid, in_specs=specs,
                out_specs=x1_out[1] + [R.rows(D), R.rows(D), R.cols(D)],
                out_shape=x1_out[0] + [R.out(D, BF16), R.out(D, BF16), jax.ShapeDtypeStruct((B, D, T), BF16)],
                compiler_params=_cp("parallel", "parallel"), name="diff_proj")(*ins)
            if has_prev:
                stream = outs[0]
            q, k, vt = outs[-3:]
            mix = [_attention("diff", q, k, vt, n_lat, n_ctx,
                              extra=(diff_lambda[j], diff_subln[j].reshape(-1, 1)), lam_init=lam_init)]
            mix_specs = [R.rows(D)]
            w_o = diff_w_o[j]
        else:
            dk = D // 2
            win = gla_w_in[j]
            zpad = jnp.zeros((D, LANES - 2 * GLA_GATE_RANK), F32)
            w = jnp.concatenate([win, zpad], axis=1).astype(BF16)
            wvt = win[:, 2 * dk:2 * dk + D].T.astype(BF16)
            wgu = jnp.zeros((LANES, 2 * dk), F32)
            wgu = wgu.at[:GLA_GATE_RANK, :dk].set(gla_w_gate_up[j, 0])
            wgu = wgu.at[GLA_GATE_RANK:2 * GLA_GATE_RANK, dk:].set(gla_w_gate_up[j, 1]).astype(BF16)
            bg = gla_b_gate[j].reshape(1, 2 * dk)
            ins = head + [w, wvt, wgu, bg]
            specs = head_specs + [R.full(a) for a in ins[len(head):]]
            vt_spec = pl.BlockSpec((1, D, R.tm), lambda b, t: (b, 0, t))
            outs = pl.pallas_call(
                functools.partial(_gla_proj_body, has_prev), grid=R.grid, in_specs=specs,
                out_specs=x1_out[1] + [R.rows(dk), R.rows(dk), R.rows(D), vt_spec, R.rows(D), R.rows(dk), R.rows(dk)],
                out_shape=x1_out[0] + [R.out(dk, F32), R.out(dk, F32), R.out(D, BF16),
                                       jax.ShapeDtypeStruct((B, D, T), BF16), R.out(D, F32),
                                       R.out(dk, F32), R.out(dk, F32)],
                compiler_params=_cp("parallel", "parallel"), name="gla_proj")(*ins)
            if has_prev:
                stream = outs[0]
            q, k, v, vt, g, af, ab = outs[-7:]
            o_f, o_b = _gla_scan(q, k, v, vt, af, ab, n_lat)
            hn = gla_head_norm[j].reshape(1, -1)
            mix = [o_f, o_b, g, hn]
            mix_specs = [R.rows(D), R.rows(D), R.rows(D), R.full(hn)]
            w_o = gla_w_o[j]

        wr = jnp.concatenate([moe_w_group[li], jnp.moveaxis(moe_w_router[li], 0, 1).reshape(D, -1)], axis=1)
        n_r = wr.shape[1]
        wr = jnp.concatenate([wr, jnp.zeros((D, LANES - n_r), F32)], axis=1).astype(BF16)
        br = jnp.concatenate([moe_b_group[li], moe_b_router[li].reshape(-1), jnp.zeros((LANES - n_r,), F32)]).reshape(1, LANES)
        tail = [mod, norm_ffn[li].reshape(1, D), w_o.astype(BF16), wr, br]
        ne = MOE_GROUPS * MOE_EXPERTS
        nt = T // R.tm
        stream, h2, gates_t = pl.pallas_call(
            functools.partial(_post_body, "gla" if kind == 2 else "attn"), grid=R.grid,
            in_specs=[R.rows(D)] + mix_specs + [R.mod(D)] + [R.full(a) for a in tail[1:]],
            out_specs=[R.rows(D), R.rows(D), pl.BlockSpec((ne, R.tm), lambda b, t: (0, b * nt + t))],
            out_shape=[R.out(D, F32), R.out(D, BF16), jax.ShapeDtypeStruct((ne, B * T), F32)],
            compiler_params=_cp("parallel", "parallel"), name="post_mixer")(stream, *mix, *tail)

        f = moe_w_gate.shape[-1]
        y = _moe(h2.reshape(B * T, D), gates_t,
                 moe_w_gate[li].reshape(ne, D, f).astype(BF16), moe_w_up[li].reshape(ne, D, f).astype(BF16),
                 jnp.swapaxes(moe_w_down[li].reshape(ne, f, D), 1, 2).astype(BF16))
        y_prev = y.reshape(B, T, D)

    RL = _Rows(B, n_lat, n_lat)
    fin = final_norm.reshape(1, D)
    return pl.pallas_call(
        _final_body, grid=RL.grid,
        in_specs=[RL.rows(D), RL.rows(D), RL.mod(D), RL.full(fin)],
        out_specs=RL.rows(D), out_shape=RL.out(D, F32),
        compiler_params=_cp("parallel", "parallel"), name="final_norm")(stream, y_prev, mods[depth - 1], fin)
```

```python
import functools
import math

import jax
import jax.numpy as jnp
from jax import lax
from jax.experimental import pallas as pl
from jax.experimental.pallas import tpu as pltpu

F32 = jnp.float32
BF16 = jnp.bfloat16

EPS = 1e-6
ROPE_BASE = 10000.0
GRID_W = 64
N_MOD = 6
LANES = 128

MLA_HEADS = 16
MLA_Q_RANK = 256
MLA_KV_RANK = 256
MLA_NOPE = 64
MLA_ROPE = 32
MLA_V = 64
DIFF_HEAD_DIM = 64
GLA_HEADS = 4
GLA_GATE_RANK = 16
GLA_GATE_NORM = 16.0
GLA_CHUNK = 128
MOE_GROUPS = 4
MOE_EXPERTS = 4

ROW_TILE = 256
MOE_TILE = 768
MOE_SLOTS = 128
MOE_PAIR = 4
ATTN_KEYS = 1024
VMEM_LIMIT = 52 * 1024 * 1024


def _cp(*sem):
    return pltpu.CompilerParams(dimension_semantics=sem, vmem_limit_bytes=VMEM_LIMIT)


def _mm(a, b):
    return jnp.dot(a, b, preferred_element_type=F32)


def _nt(a, b):
    return lax.dot_general(a, b, (((1,), (1,)), ((), ())), preferred_element_type=F32)


def _rms(x):
    return x * lax.rsqrt(jnp.mean(x * x, axis=-1, keepdims=True) + EPS)


def _silu(x):
    return x * (1.0 / (1.0 + jnp.exp(-x)))


def _ada_body(c_ref, w_ref, b_ref, o_ref):
    c = c_ref[...]
    o_ref[0] = _mm(_silu(c).astype(BF16), w_ref[0].astype(BF16)) + b_ref[0]


def _ada_all(cvec, w_ada, b_ada):
    depth, d, n6 = w_ada.shape
    tn = 1536
    return pl.pallas_call(
        _ada_body,
        grid=(depth, n6 // tn),
        in_specs=[pl.BlockSpec((8, d), lambda l, j: (0, 0)),
                  pl.BlockSpec((1, d, tn), lambda l, j: (l, 0, j)),
                  pl.BlockSpec((1, 1, tn), lambda l, j: (l, 0, j))],
        out_specs=pl.BlockSpec((1, 8, tn), lambda l, j: (l, 0, j)),
        out_shape=jax.ShapeDtypeStruct((depth, 8, n6), F32),
        compiler_params=_cp("parallel", "parallel"),
        name="adaln",
    )(cvec, w_ada, b_ada.reshape(depth, 1, n6))


class _Rows:
    def __init__(self, B, T, n_lat, tm=ROW_TILE):
        self.B, self.T, self.tm = B, T, tm
        self.n_lat_tiles = n_lat // tm
        self.grid = (B, T // tm)

    def rows(self, c):
        return pl.BlockSpec((1, self.tm, c), lambda b, t: (b, t, 0))

    def mod(self, d):
        nl, B = self.n_lat_tiles, self.B
        return pl.BlockSpec((1, N_MOD, d), lambda b, t: (jnp.where(t < nl, b, B), 0, 0))

    def cols(self, c):
        return pl.BlockSpec((1, c, self.tm), lambda b, t: (b, 0, t))

    def table(self, c):
        return pl.BlockSpec((self.tm, c), lambda b, t: (t, 0))

    @staticmethod
    def full(a):
        nd = a.ndim
        return pl.BlockSpec(a.shape, lambda b, t: (0,) * nd)

    def out(self, c, dtype):
        return jax.ShapeDtypeStruct((self.B, self.T, c), dtype)


def _prologue(refs, has_prev, k_shift):
    if has_prev:
        x_ref, y_ref, modp_ref, mod_ref, gain_ref = refs[:5]
        x = x_ref[0] + modp_ref[0, 5:6, :] * y_ref[0]
        rest = refs[5:]
    else:
        x_ref, mod_ref, gain_ref = refs[:3]
        x = x_ref[0]
        rest = refs[3:]
    h = _rms(x) * gain_ref[...]
    h = h * (1.0 + mod_ref[0, k_shift + 1:k_shift + 2, :]) + mod_ref[0, k_shift:k_shift + 1, :]
    return x, h, rest


def _mla_proj_body(has_prev, *refs):
    x, h, rest = _prologue(refs, has_prev, 0)
    (wd_ref, qn_ref, kvn_ref, wq_ref, wqs_ref, wk_ref, wvt_ref, cq_ref, sq_ref, c_ref, s_ref), outs = rest[:11], rest[11:]
    if has_prev:
        x1_ref, q_ref, k_ref, vt_ref = outs
        x1_ref[0] = x
    else:
        q_ref, k_ref, vt_ref = outs
    d = _mm(h.astype(BF16), wd_ref[...])
    cos, sin = c_ref[...], s_ref[...]
    cos_q, sin_q = cq_ref[...], sq_ref[...]
    cq = (_rms(d[:, :MLA_Q_RANK]) * qn_ref[...]).astype(BF16)
    ckv = (_rms(d[:, MLA_Q_RANK:MLA_Q_RANK + MLA_KV_RANK]) * kvn_ref[...]).astype(BF16)
    o = MLA_Q_RANK + MLA_KV_RANK
    pe = d[:, o:o + LANES] * cos + d[:, o + LANES:o + 2 * LANES] * sin
    qa = _mm(cq, wq_ref[...])
    qb = _mm(cq, wqs_ref[...])
    kn = _mm(ckv, wk_ref[...])
    for hd in range(MLA_HEADS):
        sl = slice(hd * LANES, (hd + 1) * LANES)
        q_ref[0, :, sl] = (qa[:, sl] * cos_q + qb[:, sl] * sin_q).astype(BF16)
        k_ref[0, :, sl] = (kn[:, sl] + pe).astype(BF16)
    vt_ref[0] = _nt(wvt_ref[...], ckv).astype(BF16)


def _diff_proj_body(has_prev, *refs):
    x, h, rest = _prologue(refs, has_prev, 0)
    (w_ref, wvt_ref, cq_ref, sq_ref, ck_ref, sk_ref), outs = rest[:6], rest[6:]
    if has_prev:
        x1_ref, q_ref, k_ref, vt_ref = outs
        x1_ref[0] = x
    else:
        q_ref, k_ref, vt_ref = outs
    dm = h.shape[-1]
    hb = h.astype(BF16)
    p = _mm(hb, w_ref[...])
    cq, sq, ck, sk = cq_ref[...], sq_ref[...], ck_ref[...], sk_ref[...]
    for hd in range(dm // LANES):
        sl = slice(hd * LANES, (hd + 1) * LANES)
        q_ref[0, :, sl] = (p[:, sl] * cq + p[:, 2 * dm + hd * LANES:2 * dm + (hd + 1) * LANES] * sq).astype(BF16)
        k_ref[0, :, sl] = (p[:, dm + hd * LANES:dm + (hd + 1) * LANES] * ck
                           + p[:, 3 * dm + hd * LANES:3 * dm + (hd + 1) * LANES] * sk).astype(BF16)
    vt_ref[0] = _nt(wvt_ref[...], hb).astype(BF16)


def _gla_proj_body(has_prev, *refs):
    x, h, rest = _prologue(refs, has_prev, 0)
    (w_ref, wvt_ref, wgu_ref, bg_ref), outs = rest[:4], rest[4:]
    if has_prev:
        x1_ref, q_ref, k_ref, v_ref, vt_ref, g_ref, af_ref, ab_ref = outs
        x1_ref[0] = x
    else:
        q_ref, k_ref, v_ref, vt_ref, g_ref, af_ref, ab_ref = outs
    dm = h.shape[-1]
    dk = dm // 2
    hb = h.astype(BF16)
    p = _mm(hb, w_ref[...])
    q_ref[0] = p[:, :dk] * ((dk // GLA_HEADS) ** -0.5)
    k_ref[0] = p[:, dk:2 * dk]
    v_ref[0] = p[:, 2 * dk:2 * dk + dm].astype(BF16)
    vt_ref[0] = _nt(wvt_ref[...], hb).astype(BF16)
    g_ref[0] = p[:, 2 * dk + dm:2 * dk + 2 * dm]
    z = p[:, 2 * dk + 2 * dm:].astype(BF16)
    a = _mm(z, wgu_ref[...]) + bg_ref[...]
    ls = (jnp.minimum(a, 0.0) - jnp.log(1.0 + jnp.exp(-jnp.abs(a)))) / GLA_GATE_NORM
    af_ref[0] = ls[:, :dk]
    ab_ref[0] = ls[:, dk:]


def _route(logits_t):
    def col(i):
        return logits_t[i:i + 1, :]

    lg = [col(i) for i in range(MOE_GROUPS)]
    gmax = functools.reduce(jnp.maximum, lg)
    onehot, taken = [], None
    for i in range(MOE_GROUPS):
        hit = lg[i] == gmax
        if taken is not None:
            hit = jnp.logical_and(hit, jnp.logical_not(taken))
        taken = hit if taken is None else jnp.logical_or(taken, hit)
        onehot.append(hit)
    grp_w = 1.0 / functools.reduce(lambda a, b: a + b, [jnp.exp(l - gmax) for l in lg])
    le = []
    for e in range(MOE_EXPERTS):
        v = None
        for g in range(MOE_GROUPS):
            c = jnp.where(onehot[g], col(MOE_GROUPS + g * MOE_EXPERTS + e), 0.0)
            v = c if v is None else v + c
        le.append(v)
    vmax = functools.reduce(jnp.maximum, le)
    num = []
    for e in range(MOE_EXPERTS):
        rank = None
        for o in range(MOE_EXPERTS):
            if o == e:
                continue
            ahead = (le[o] >= le[e]) if o < e else (le[o] > le[e])
            r = jnp.where(ahead, 1.0, 0.0)
            rank = r if rank is None else rank + r
        num.append(jnp.where(rank < 2.0, jnp.exp(le[e] - vmax), 0.0))
    scale = grp_w / functools.reduce(lambda a, b: a + b, num)
    return [jnp.where(onehot[g], num[e] * scale, 0.0) for g in range(MOE_GROUPS) for e in range(MOE_EXPERTS)]


def _post_body(kind, *refs):
    x1_ref = refs[0]
    if kind == "gla":
        of_ref, ob_ref, g_ref, hn_ref = refs[1:5]
        rest = refs[5:]
        o = of_ref[0] + ob_ref[0]
        hn = hn_ref[...]
        dv = hn.shape[-1]
        parts = [_rms(o[:, i * dv:(i + 1) * dv]) * hn for i in range(o.shape[-1] // dv)]
        a = (jnp.concatenate(parts, axis=-1) * _silu(g_ref[0])).astype(BF16)
    else:
        a = refs[1][0]
        rest = refs[2:]
    mod_ref, gain_ref, wo_ref, wr_ref, br_ref, x2_ref, h2_ref, gate_ref = rest
    x2 = x1_ref[0] + mod_ref[0, 2:3, :] * _mm(a, wo_ref[...])
    x2_ref[0] = x2
    h2 = _rms(x2) * gain_ref[...]
    h2 = (h2 * (1.0 + mod_ref[0, 4:5, :]) + mod_ref[0, 3:4, :]).astype(BF16)
    h2_ref[0] = h2
    for i, row in enumerate(_route((_mm(h2, wr_ref[...]) + br_ref[...]).T)):
        gate_ref[i:i + 1, :] = row


def _final_body(x_ref, y_ref, mod_ref, gain_ref, o_ref):
    x = x_ref[0] + mod_ref[0, 5:6, :] * y_ref[0]
    o_ref[0] = _rms(x) * gain_ref[...]


def _attn_body(mode, n_lat, n_ctx, tk, lam_init, *refs):
    if mode == "mla":
        q_ref, k_ref, vt_ref, o_ref, s_sc, m_sc, l_sc, acc_sc = refs
    else:
        q_ref, k_ref, vt_ref, lam_ref, sub_ref, o_ref, s_sc, m_sc, l_sc, acc_sc = refs
    tq = q_ref.shape[1]
    dv = acc_sc.shape[1]
    n_steps = (n_lat + n_ctx) // tk
    is_lat = pl.program_id(2) < (n_lat // tq)

    def q_of(i):
        if mode == "mla":
            return q_ref[0, :, i * LANES:(i + 1) * LANES]
        q = q_ref[0]
        lane = lax.broadcasted_iota(jnp.int32, q.shape, 1)
        keep = (lane < DIFF_HEAD_DIM) if i == 0 else (lane >= DIFF_HEAD_DIM)
        return jnp.where(keep, q, jnp.zeros_like(q))

    qs = [q_of(0), q_of(1)]

    def scores(i, rows):
        k = k_ref[0, rows, i * LANES:(i + 1) * LANES] if mode == "mla" else k_ref[0, rows, :]
        return _nt(k, qs[i])

    def vt_of(i, cols):
        return vt_ref[0, i * dv:(i + 1) * dv, cols] if mode == "mla" else vt_ref[0, :, cols]

    def absorb(i, s, cols, first=False):
        m_new = jnp.max(s, axis=0, keepdims=True)
        if not first:
            m_old = m_sc[i]
            m_new = jnp.maximum(m_old, m_new)
            alpha = jnp.exp2(m_old - m_new)
        p = jnp.exp2(s - m_new)
        l = jnp.sum(p, axis=0, keepdims=True)
        pv = _mm(vt_of(i, cols), p.astype(BF16))
        l_sc[i] = l if first else alpha * l_sc[i] + l
        acc_sc[i] = pv if first else alpha * acc_sc[i] + pv
        m_sc[i] = m_new

    @pl.when(jnp.logical_not(is_lat))
    def _():
        ctx = slice(n_lat, n_lat + n_ctx)
        for i in range(2):
            absorb(i, scores(i, ctx), ctx, first=True)

    @pl.when(is_lat)
    def _():
        for i in range(2):
            s_sc[0, i] = scores(i, slice(0, tk))
        for c in range(n_steps):
            slot = c % 2
            for i in range(2):
                if c + 1 < n_steps:
                    s_next = scores(i, slice((c + 1) * tk, (c + 2) * tk))
                absorb(i, s_sc[slot, i], slice(c * tk, (c + 1) * tk), first=(c == 0))
                if c + 1 < n_steps:
                    s_sc[1 - slot, i] = s_next

    o0 = acc_sc[0] * (1.0 / l_sc[0])
    o1 = acc_sc[1] * (1.0 / l_sc[1])
    if mode == "mla":
        o = jnp.concatenate([o0, o1], axis=0)
    else:
        lam = lam_ref[...]
        lam_full = (jnp.exp(jnp.sum(lam[0:1] * lam[1:2], axis=-1, keepdims=True))
                    - jnp.exp(jnp.sum(lam[2:3] * lam[3:4], axis=-1, keepdims=True)) + lam_init)
        o = o0 - lam_full * o1
        o = o * lax.rsqrt(jnp.mean(o * o, axis=0, keepdims=True) + EPS) * sub_ref[...] * (1.0 - lam_init)
    o_ref[0] = o.T.astype(BF16)


def _attention(mode, q, k, vt, n_lat, n_ctx, extra=(), lam_init=0.0):
    B, T, _ = q.shape
    tq = ROW_TILE
    tk = max(t for t in range(LANES, ATTN_KEYS + 1, LANES) if T % t == 0)
    qw = 2 * LANES if mode == "mla" else LANES
    dv = MLA_V if mode == "mla" else LANES
    nh = q.shape[-1] // qw
    in_specs = [pl.BlockSpec((1, tq, qw), lambda b, h, i: (b, i, h)),
                pl.BlockSpec((1, T, qw), lambda b, h, i: (b, 0, h)),
                pl.BlockSpec((1, LANES, T), lambda b, h, i: (b, h, 0))]
    for a in extra:
        in_specs.append(pl.BlockSpec(a.shape, lambda b, h, i, nd=a.ndim: (0,) * nd))
    return pl.pallas_call(
        functools.partial(_attn_body, mode, n_lat, n_ctx, tk, lam_init),
        grid=(B, nh, T // tq),
        in_specs=in_specs,
        out_specs=pl.BlockSpec((1, tq, LANES), lambda b, h, i: (b, i, h)),
        out_shape=jax.ShapeDtypeStruct((B, T, nh * LANES), BF16),
        scratch_shapes=[pltpu.VMEM((2, 2, tk, tq), F32), pltpu.VMEM((2, 1, tq), F32),
                        pltpu.VMEM((2, 1, tq), F32), pltpu.VMEM((2, dv, tq), F32)],
        compiler_params=_cp("parallel", "parallel", "arbitrary"),
        name="attn_" + mode,
    )(q, k, vt, *extra)


def _cumsum_rows(tri, x):
    hi = x.astype(BF16)
    r1 = x - hi.astype(F32)
    mid = r1.astype(BF16)
    lo = (r1 - mid.astype(F32)).astype(BF16)
    return _mm(tri, hi) + _mm(tri, mid) + _mm(tri, lo)


def _gla_scan_body(n_chunks, *refs):
    (qf_ref, kf_ref, vf_ref, vtf_ref, af_ref, qb_ref, kb_ref, vb_ref, vtb_ref, ab_ref,
     of_ref, ob_ref, sf_sc, sb_sc) = refs
    C = GLA_CHUNK
    n_heads, dvh, dkh = sf_sc.shape

    @pl.when(pl.program_id(1) == 0)
    def _():
        sf_sc[...] = jnp.zeros_like(sf_sc)
        sb_sc[...] = jnp.zeros_like(sb_sc)

    r = lax.broadcasted_iota(jnp.int32, (C, C), 0)
    cidx = lax.broadcasted_iota(jnp.int32, (C, C), 1)
    lower = cidx <= r

    def chunk(ci, hd, q_ref, k_ref, v_ref, vt_ref, a_ref, o_ref, s_sc, reverse):
        rows = slice(ci * C, (ci + 1) * C)
        kl = slice(hd * dkh, (hd + 1) * dkh)
        vl = slice(hd * dvh, (hd + 1) * dvh)
        keep = jnp.logical_not(lower) | (cidx == r) if reverse else lower
        tri = jnp.where(keep, 1.0, 0.0).astype(BF16)
        cum = _cumsum_rows(tri, a_ref[0, rows, kl])
        total = cum[0:1, :] if reverse else cum[C - 1:C, :]
        q = q_ref[0, rows, kl]
        k = k_ref[0, rows, kl]
        qd = (q * jnp.exp(cum)).astype(BF16)
        ki = (k * jnp.exp(-cum)).astype(BF16)
        kd = (k * jnp.exp(total - cum)).astype(BF16)
        att = jnp.where(keep, _nt(qd, ki), 0.0).astype(BF16)
        st = s_sc[hd]
        o_ref[0, rows, vl] = _nt(qd, st.astype(BF16)) + _mm(att, v_ref[0, rows, vl])
        s_sc[hd] = st * jnp.exp(total) + _mm(vt_ref[0, vl, rows], kd)

    for ci in range(n_chunks):
        for hd in range(n_heads):
            chunk(ci, hd, qf_ref, kf_ref, vf_ref, vtf_ref, af_ref, of_ref, sf_sc, False)
            chunk(n_chunks - 1 - ci, hd, qb_ref, kb_ref, vb_ref, vtb_ref, ab_ref, ob_ref, sb_sc, True)


def _gla_scan(q, k, v, vt, af, ab, n_lat):
    B, T, dk = q.shape
    dm = v.shape[-1]
    dkh, dvh = dk // GLA_HEADS, dm // GLA_HEADS
    lb = 2 * GLA_CHUNK
    nb = T // lb
    n_lat_b = n_lat // lb
    n_ctx_b = nb - n_lat_b

    def fwd(j):
        return jnp.where(j < n_ctx_b, n_lat_b + j, j - n_ctx_b)

    def bwd(j):
        return jnp.where(j < n_ctx_b, nb - 1 - j, nb - 1 - j)

    def specs(blk):
        return [pl.BlockSpec((1, lb, dk), lambda b, j: (b, blk(j), 0)),
                pl.BlockSpec((1, lb, dk), lambda b, j: (b, blk(j), 0)),
                pl.BlockSpec((1, lb, dm), lambda b, j: (b, blk(j), 0)),
                pl.BlockSpec((1, dm, lb), lambda b, j: (b, 0, blk(j))),
                pl.BlockSpec((1, lb, dk), lambda b, j: (b, blk(j), 0))]

    o_sds = jax.ShapeDtypeStruct((B, T, dm), F32)
    state = pltpu.VMEM((GLA_HEADS, dvh, dkh), F32)
    return pl.pallas_call(
        functools.partial(_gla_scan_body, lb // GLA_CHUNK),
        grid=(B, nb),
        in_specs=specs(fwd) + specs(bwd),
        out_specs=[pl.BlockSpec((1, lb, dm), lambda b, j: (b, fwd(j), 0)),
                   pl.BlockSpec((1, lb, dm), lambda b, j: (b, bwd(j), 0))],
        out_shape=[o_sds, o_sds],
        scratch_shapes=[state, state],
        compiler_params=_cp("parallel", "arbitrary"),
        name="gla_scan",
    )(q, k, v, vt, af, q, k, v, vt, ab)


def _moe_body(ne, h_ref, g_ref, tri_ref, wg_ref, wu_ref, wdt_ref, y_ref, key_sc, gt_sc, yt_sc):
    e = pl.program_id(1)
    tm = h_ref.shape[0]
    C = MOE_SLOTS

    @pl.when(e == 0)
    def _():
        gt = g_ref[...]
        sel = gt != 0.0
        rank = _mm(jnp.where(sel, 1.0, 0.0).astype(BF16), tri_ref[...])
        key_sc[...] = jnp.where(sel, rank, -1.0)
        gt_sc[...] = gt
        yt_sc[...] = jnp.zeros_like(yt_sc)

    keys = [key_sc[pl.ds(e * MOE_PAIR + k, 1), :] for k in range(MOE_PAIR)]
    gates = [gt_sc[pl.ds(e * MOE_PAIR + k, 1), :] for k in range(MOE_PAIR)]

    def select(k, r):
        slot = (lax.broadcasted_iota(jnp.int32, (C, tm), 0) + r * C).astype(F32)
        hit = keys[k] == slot
        return jnp.where(hit, 1.0, 0.0).astype(BF16), jnp.where(hit, gates[k], 0.0).astype(BF16)

    def expert(k, xg):
        a = _mm(xg, wg_ref[k])
        u = _mm(xg, wu_ref[k])
        act = (_silu(a) * u).astype(BF16)
        return _nt(wdt_ref[k], act).astype(BF16)

    def run(ks, r):
        sel = [select(k, r) for k in ks]
        xg = _mm(jnp.concatenate([p for p, _ in sel], axis=0), h_ref[...]).astype(BF16)
        zt = [expert(k, xg[i * C:(i + 1) * C]) for i, k in enumerate(ks)]
        yt_sc[...] += _mm(jnp.concatenate(zt, axis=1), jnp.concatenate([pg for _, pg in sel], axis=0))

    run(list(range(MOE_PAIR)), 0)
    for k in range(MOE_PAIR):
        n_chunks = (jnp.max(keys[k]).astype(jnp.int32) + C) // C

        def more(r, carry, k=k):
            run([k], r)
            return carry

        lax.fori_loop(1, n_chunks, more, 0)

    @pl.when(e == ne // MOE_PAIR - 1)
    def _():
        y_ref[...] = yt_sc[...].T


def _moe(h2, gates, wg, wu, wdt):
    n, d = h2.shape
    ne, _, f = wg.shape
    tm = math.gcd(n, MOE_TILE)
    tri = jnp.triu(jnp.ones((tm, tm), F32), 1).astype(BF16)
    return pl.pallas_call(
        functools.partial(_moe_body, ne),
        grid=(n // tm, ne // MOE_PAIR),
        in_specs=[pl.BlockSpec((tm, d), lambda t, e: (t, 0)),
                  pl.BlockSpec((ne, tm), lambda t, e: (0, t)),
                  pl.BlockSpec((tm, tm), lambda t, e: (0, 0)),
                  pl.BlockSpec((MOE_PAIR, d, f), lambda t, e: (e, 0, 0)),
                  pl.BlockSpec((MOE_PAIR, d, f), lambda t, e: (e, 0, 0)),
                  pl.BlockSpec((MOE_PAIR, d, f), lambda t, e: (e, 0, 0))],
        out_specs=pl.BlockSpec((tm, d), lambda t, e: (t, 0)),
        out_shape=jax.ShapeDtypeStruct((n, d), F32),
        scratch_shapes=[pltpu.VMEM((ne, tm), F32), pltpu.VMEM((ne, tm), F32), pltpu.VMEM((d, tm), F32)],
        compiler_params=_cp("parallel", "arbitrary"),
        name="moe",
    )(h2, gates, tri, wg, wu, wdt)


def _rope_angles(n_lat, n_ctx, rot_dim):
    rows = n_lat // GRID_W
    row = jnp.repeat(jnp.arange(rows, dtype=F32), GRID_W)
    col = jnp.tile(jnp.arange(GRID_W, dtype=F32), rows)
    n_freq = rot_dim // 4
    inv_freq = ROPE_BASE ** (-jnp.arange(n_freq, dtype=F32) / n_freq)
    ang = jnp.concatenate([row[:, None] * inv_freq, col[:, None] * inv_freq], axis=-1)
    cos = jnp.concatenate([jnp.cos(ang), jnp.ones((n_ctx, rot_dim // 2), F32)], axis=0)
    sin = jnp.concatenate([jnp.sin(ang), jnp.zeros((n_ctx, rot_dim // 2), F32)], axis=0)
    return cos, sin


def _swap_halves(w, width):
    lead = w.shape[:-1]
    w = w.reshape(*lead, -1, 2, width // 2)
    return jnp.flip(w, axis=-2).reshape(*lead, -1)


def kernel(x, c, ctx, c_ctx, w_ada, b_ada, norm_mix, norm_ffn, mla_w_dqkv, mla_q_norm, mla_w_uq, mla_kv_norm, mla_w_ukv, mla_w_o, diff_w_qkv, diff_lambda, diff_subln, diff_w_o, gla_w_in, gla_w_gate_up, gla_b_gate, gla_head_norm, gla_w_o, moe_w_group, moe_b_group, moe_w_router, moe_b_router, moe_w_gate, moe_w_up, moe_w_down, final_norm):
    B, n_lat, D = x.shape
    n_ctx = ctx.shape[1]
    T = n_lat + n_ctx
    depth = w_ada.shape[0]
    assert n_lat % ROW_TILE == 0 and n_ctx % ROW_TILE == 0 and B + 1 <= 8 and D % LANES == 0
    R = _Rows(B, T, n_lat)

    cvec = jnp.zeros((8, D), F32).at[:B].set(c).at[B].set(c_ctx)
    mods = _ada_all(cvec, w_ada, b_ada).reshape(depth, 8, N_MOD, D)

    stream = jnp.concatenate([x, ctx], axis=1)
    y_prev = None

    cos_a, sin_a = _rope_angles(n_lat, n_ctx, MLA_ROPE)
    ones, zeros = jnp.ones((T, MLA_NOPE), F32), jnp.zeros((T, LANES - MLA_NOPE - MLA_ROPE), F32)
    mla_cos = jnp.concatenate([ones, cos_a, cos_a, zeros], axis=1)
    mla_sin = jnp.concatenate([0 * ones, -sin_a, sin_a, zeros], axis=1)
    cos_b, sin_b = _rope_angles(n_lat, n_ctx, DIFF_HEAD_DIM)
    diff_cos = jnp.concatenate([cos_b] * 4, axis=1)
    diff_sin = jnp.concatenate([-sin_b, sin_b] * 2, axis=1)
    diff_scale = DIFF_HEAD_DIM ** -0.5 * math.log2(math.e)
    mla_qscale = (MLA_NOPE + MLA_ROPE) ** -0.5 * math.log2(math.e)

    for li in range(depth):
        last = li == depth - 1
        kind, j = li % 3, li // 3
        mod = mods[li]
        has_prev = y_prev is not None
        head = [stream] + ([y_prev, mods[li - 1]] if has_prev else []) + [mod, norm_mix[li].reshape(1, D)]
        head_specs = [R.rows(D)] + ([R.rows(D), R.mod(D)] if has_prev else []) + [R.mod(D), R.full(head[-1])]
        x1_out = ([R.out(D, F32)], [R.rows(D)]) if has_prev else ([], [])

        if kind == 0:
            wd = mla_w_dqkv[j]
            pe_w = wd[:, MLA_Q_RANK + MLA_KV_RANK:]
            padl, padr = jnp.zeros((D, MLA_NOPE), F32), jnp.zeros((D, LANES - MLA_NOPE - MLA_ROPE), F32)
            wdown = jnp.concatenate([wd[:, :MLA_Q_RANK + MLA_KV_RANK], padl, pe_w, padr,
                                     padl, _swap_halves(pe_w, MLA_ROPE), padr], axis=1).astype(BF16)
            wq = mla_w_uq[j].reshape(MLA_Q_RANK, MLA_HEADS, MLA_NOPE + MLA_ROPE)
            pad = jnp.zeros((MLA_Q_RANK, MLA_HEADS, LANES - MLA_NOPE - MLA_ROPE), F32)
            wq_a = jnp.concatenate([wq, pad], axis=-1).reshape(MLA_Q_RANK, -1).astype(BF16)
            wq_s = jnp.concatenate([0 * wq[..., :MLA_NOPE], _swap_halves(wq[..., MLA_NOPE:], MLA_ROPE), pad],
                                   axis=-1).reshape(MLA_Q_RANK, -1).astype(BF16)
            wkv = mla_w_ukv[j].reshape(MLA_KV_RANK, MLA_HEADS, MLA_NOPE + MLA_V)
            wk = jnp.concatenate([wkv[..., :MLA_NOPE], jnp.zeros((MLA_KV_RANK, MLA_HEADS, LANES - MLA_NOPE), F32)],
                                 axis=-1).reshape(MLA_KV_RANK, -1).astype(BF16)
            wvt = wkv[..., MLA_NOPE:].reshape(MLA_KV_RANK, -1).T.astype(BF16)
            ins = head + [wdown, mla_q_norm[j].reshape(1, -1), mla_kv_norm[j].reshape(1, -1),
                          wq_a, wq_s, wk, wvt, mla_cos * mla_qscale, mla_sin * mla_qscale, mla_cos, mla_sin]
            specs = head_specs + [R.full(a) for a in ins[len(head):-4]] + [R.table(LANES)] * 4
            outs = pl.pallas_call(
                functools.partial(_mla_proj_body, has_prev), grid=R.grid, in_specs=specs,
                out_specs=x1_out[1] + [R.rows(MLA_HEADS * LANES), R.rows(MLA_HEADS * LANES), R.cols(MLA_HEADS * MLA_V)],
                out_shape=x1_out[0] + [R.out(MLA_HEADS * LANES, BF16), R.out(MLA_HEADS * LANES, BF16),
                                       jax.ShapeDtypeStruct((B, MLA_HEADS * MLA_V, T), BF16)],
                compiler_params=_cp("parallel", "parallel"), name="mla_proj")(*ins)
            if has_prev:
                stream = outs[0]
            q, k, vt = outs[-3:]
            mix = [_attention("mla", q, k, vt, n_lat, n_ctx)]
            mix_specs = [R.rows(D)]
            w_o = mla_w_o[j]
        elif kind == 1:
            lam_init = 0.8 - 0.6 * math.exp(-0.3 * li)
            wq, wk, wv = jnp.split(diff_w_qkv[j], 3, axis=1)
            w = jnp.concatenate([wq, wk, _swap_halves(wq, DIFF_HEAD_DIM), _swap_halves(wk, DIFF_HEAD_DIM)],
                                axis=1).astype(BF16)
            wvt = wv.T.astype(BF16)
            ins = head + [w, wvt, diff_cos * diff_scale, diff_sin * diff_scale, diff_cos, diff_sin]
            specs = head_specs + [R.full(w), R.full(wvt)] + [R.table(LANES)] * 4
            outs = pl.pallas_call(
                functools.partial(_diff_proj_body, has_prev), grid=R.grid, in_specs=specs,
                out_specs=x1_out[1] + [R.rows(D), R.rows(D), R.cols(D)],
                out_shape=x1_out[0] + [R.out(D, BF16), R.out(D, BF16), jax.ShapeDtypeStruct((B, D, T), BF16)],
                compiler_params=_cp("parallel", "parallel"), name="diff_proj")(*ins)
            if has_prev:
                stream = outs[0]
            q, k, vt = outs[-3:]
            mix = [_attention("diff", q, k, vt, n_lat, n_ctx,
                              extra=(diff_lambda[j], diff_subln[j].reshape(-1, 1)), lam_init=lam_init)]
            mix_specs = [R.rows(D)]
            w_o = diff_w_o[j]
        else:
            dk = D // 2
            win = gla_w_in[j]
            zpad = jnp.zeros((D, LANES - 2 * GLA_GATE_RANK), F32)
            w = jnp.concatenate([win, zpad], axis=1).astype(BF16)
            wvt = win[:, 2 * dk:2 * dk + D].T.astype(BF16)
            wgu = jnp.zeros((LANES, 2 * dk), F32)
            wgu = wgu.at[:GLA_GATE_RANK, :dk].set(gla_w_gate_up[j, 0])
            wgu = wgu.at[GLA_GATE_RANK:2 * GLA_GATE_RANK, dk:].set(gla_w_gate_up[j, 1]).astype(BF16)
            bg = gla_b_gate[j].reshape(1, 2 * dk)
            ins = head + [w, wvt, wgu, bg]
            specs = head_specs + [R.full(a) for a in ins[len(head):]]
            vt_spec = pl.BlockSpec((1, D, R.tm), lambda b, t: (b, 0, t))
            outs = pl.pallas_call(
                functools.partial(_gla_proj_body, has_prev), grid=R.grid, in_specs=specs,
                out_specs=x1_out[1] + [R.rows(dk), R.rows(dk), R.rows(D), vt_spec, R.rows(D), R.rows(dk), R.rows(dk)],
                out_shape=x1_out[0] + [R.out(dk, F32), R.out(dk, F32), R.out(D, BF16),
                                       jax.ShapeDtypeStruct((B, D, T), BF16), R.out(D, F32),
                                       R.out(dk, F32), R.out(dk, F32)],
                compiler_params=_cp("parallel", "parallel"), name="gla_proj")(*ins)
            if has_prev:
                stream = outs[0]
            q, k, v, vt, g, af, ab = outs[-7:]
            o_f, o_b = _gla_scan(q, k, v, vt, af, ab, n_lat)
            hn = gla_head_norm[j].reshape(1, -1)
            mix = [o_f, o_b, g, hn]
            mix_specs = [R.rows(D), R.rows(D), R.rows(D), R.full(hn)]
            w_o = gla_w_o[j]

        wr = jnp.concatenate([moe_w_group[li], jnp.moveaxis(moe_w_router[li], 0, 1).reshape(D, -1)], axis=1)
        n_r = wr.shape[1]
        wr = jnp.concatenate([wr, jnp.zeros((D, LANES - n_r), F32)], axis=1).astype(BF16)
        br = jnp.concatenate([moe_b_group[li], moe_b_router[li].reshape(-1), jnp.zeros((LANES - n_r,), F32)]).reshape(1, LANES)
        tail = [mod, norm_ffn[li].reshape(1, D), w_o.astype(BF16), wr, br]
        ne = MOE_GROUPS * MOE_EXPERTS
        nt = T // R.tm
        stream, h2, gates_t = pl.pallas_call(
            functools.partial(_post_body, "gla" if kind == 2 else "attn"), grid=R.grid,
            in_specs=[R.rows(D)] + mix_specs + [R.mod(D)] + [R.full(a) for a in tail[1:]],
            out_specs=[R.rows(D), R.rows(D), pl.BlockSpec((ne, R.tm), lambda b, t: (0, b * nt + t))],
            out_shape=[R.out(D, F32), R.out(D, BF16), jax.ShapeDtypeStruct((ne, B * T), F32)],
            compiler_params=_cp("parallel", "parallel"), name="post_mixer")(stream, *mix, *tail)

        f = moe_w_gate.shape[-1]
        y = _moe(h2.reshape(B * T, D), gates_t,
                 moe_w_gate[li].reshape(ne, D, f).astype(BF16), moe_w_up[li].reshape(ne, D, f).astype(BF16),
                 jnp.swapaxes(moe_w_down[li].reshape(ne, f, D), 1, 2).astype(BF16))
        y_prev = y.reshape(B, T, D)

    RL = _Rows(B, n_lat, n_lat)
    fin = final_norm.reshape(1, D)
    return pl.pallas_call(
        _final_body, grid=RL.grid,
        in_specs=[RL.rows(D), RL.rows(D), RL.mod(D), RL.full(fin)],
        out_specs=RL.rows(D), out_shape=RL.out(D, F32),
        compiler_params=_cp("parallel", "parallel"), name="final_norm")(stream, y_prev, mods[depth - 1], fin)
```

```python
import functools
import math

import jax
import jax.numpy as jnp
from jax import lax
from jax.experimental import pallas as pl
from jax.experimental.pallas import tpu as pltpu

F32 = jnp.float32
BF16 = jnp.bfloat16

EPS = 1e-6
ROPE_BASE = 10000.0
GRID_W = 64
N_MOD = 6
LANES = 128

MLA_HEADS = 16
MLA_Q_RANK = 256
MLA_KV_RANK = 256
MLA_NOPE = 64
MLA_ROPE = 32
MLA_V = 64
DIFF_HEAD_DIM = 64
GLA_HEADS = 4
GLA_GATE_RANK = 16
GLA_GATE_NORM = 16.0
GLA_CHUNK = 128
MOE_GROUPS = 4
MOE_EXPERTS = 4

ROW_TILE = 256
MOE_TILE = 768
MOE_SLOTS = 128
MOE_PAIR = 4
ATTN_KEYS = 1024
ATTN_Q = 512
VMEM_LIMIT = 52 * 1024 * 1024


def _cp(*sem):
    return pltpu.CompilerParams(dimension_semantics=sem, vmem_limit_bytes=VMEM_LIMIT)


def _mm(a, b):
    return jnp.dot(a, b, preferred_element_type=F32)


def _nt(a, b):
    return lax.dot_general(a, b, (((1,), (1,)), ((), ())), preferred_element_type=F32)


def _rms(x):
    return x * lax.rsqrt(jnp.mean(x * x, axis=-1, keepdims=True) + EPS)


def _silu(x):
    return x * (1.0 / (1.0 + jnp.exp(-x)))


def _ada_body(c_ref, w_ref, b_ref, o_ref):
    c = c_ref[...]
    o_ref[0] = _mm(_silu(c).astype(BF16), w_ref[0].astype(BF16)) + b_ref[0]


def _ada_all(cvec, w_ada, b_ada):
    depth, d, n6 = w_ada.shape
    tn = 1536
    return pl.pallas_call(
        _ada_body,
        grid=(depth, n6 // tn),
        in_specs=[pl.BlockSpec((8, d), lambda l, j: (0, 0)),
                  pl.BlockSpec((1, d, tn), lambda l, j: (l, 0, j)),
                  pl.BlockSpec((1, 1, tn), lambda l, j: (l, 0, j))],
        out_specs=pl.BlockSpec((1, 8, tn), lambda l, j: (l, 0, j)),
        out_shape=jax.ShapeDtypeStruct((depth, 8, n6), F32),
        compiler_params=_cp("parallel", "parallel"),
        name="adaln",
    )(cvec, w_ada, b_ada.reshape(depth, 1, n6))


class _Rows:
    def __init__(self, B, T, n_lat, tm=ROW_TILE):
        self.B, self.T, self.tm = B, T, tm
        self.n_lat_tiles = n_lat // tm
        self.grid = (B, T // tm)

    def rows(self, c):
        return pl.BlockSpec((1, self.tm, c), lambda b, t: (b, t, 0))

    def mod(self, d):
        nl, B = self.n_lat_tiles, self.B
        return pl.BlockSpec((1, N_MOD, d), lambda b, t: (jnp.where(t < nl, b, B), 0, 0))

    def cols(self, c):
        return pl.BlockSpec((1, c, self.tm), lambda b, t: (b, 0, t))

    def table(self, c):
        return pl.BlockSpec((self.tm, c), lambda b, t: (t, 0))

    @staticmethod
    def full(a):
        nd = a.ndim
        return pl.BlockSpec(a.shape, lambda b, t: (0,) * nd)

    def out(self, c, dtype):
        return jax.ShapeDtypeStruct((self.B, self.T, c), dtype)


def _prologue(refs, has_prev, k_shift):
    if has_prev:
        x_ref, y_ref, modp_ref, mod_ref, gain_ref = refs[:5]
        x = x_ref[0] + modp_ref[0, 5:6, :] * y_ref[0]
        rest = refs[5:]
    else:
        x_ref, mod_ref, gain_ref = refs[:3]
        x = x_ref[0]
        rest = refs[3:]
    h = _rms(x) * gain_ref[...]
    h = h * (1.0 + mod_ref[0, k_shift + 1:k_shift + 2, :]) + mod_ref[0, k_shift:k_shift + 1, :]
    return x, h, rest


def _mla_proj_body(has_prev, *refs):
    x, h, rest = _prologue(refs, has_prev, 0)
    (wd_ref, qn_ref, kvn_ref, wq_ref, wqs_ref, wk_ref, wvt_ref, cq_ref, sq_ref, c_ref, s_ref), outs = rest[:11], rest[11:]
    if has_prev:
        x1_ref, q_ref, k_ref, vt_ref = outs
        x1_ref[0] = x
    else:
        q_ref, k_ref, vt_ref = outs
    d = _mm(h.astype(BF16), wd_ref[...])
    cos, sin = c_ref[...], s_ref[...]
    cos_q, sin_q = cq_ref[...], sq_ref[...]
    cq = (_rms(d[:, :MLA_Q_RANK]) * qn_ref[...]).astype(BF16)
    ckv = (_rms(d[:, MLA_Q_RANK:MLA_Q_RANK + MLA_KV_RANK]) * kvn_ref[...]).astype(BF16)
    o = MLA_Q_RANK + MLA_KV_RANK
    pe = d[:, o:o + LANES] * cos + d[:, o + LANES:o + 2 * LANES] * sin
    qa = _mm(cq, wq_ref[...])
    qb = _mm(cq, wqs_ref[...])
    kn = _mm(ckv, wk_ref[...])
    for hd in range(MLA_HEADS):
        sl = slice(hd * LANES, (hd + 1) * LANES)
        q_ref[0, :, sl] = (qa[:, sl] * cos_q + qb[:, sl] * sin_q).astype(BF16)
        k_ref[0, :, sl] = (kn[:, sl] + pe).astype(BF16)
    vt_ref[0] = _nt(wvt_ref[...], ckv).astype(BF16)


def _diff_proj_body(has_prev, *refs):
    x, h, rest = _prologue(refs, has_prev, 0)
    (w_ref, wvt_ref, cq_ref, sq_ref, ck_ref, sk_ref), outs = rest[:6], rest[6:]
    if has_prev:
        x1_ref, q_ref, k_ref, vt_ref = outs
        x1_ref[0] = x
    else:
        q_ref, k_ref, vt_ref = outs
    dm = h.shape[-1]
    hb = h.astype(BF16)
    p = _mm(hb, w_ref[...])
    cq, sq, ck, sk = cq_ref[...], sq_ref[...], ck_ref[...], sk_ref[...]
    for hd in range(dm // LANES):
        sl = slice(hd * LANES, (hd + 1) * LANES)
        q_ref[0, :, sl] = (p[:, sl] * cq + p[:, 2 * dm + hd * LANES:2 * dm + (hd + 1) * LANES] * sq).astype(BF16)
        k_ref[0, :, sl] = (p[:, dm + hd * LANES:dm + (hd + 1) * LANES] * ck
                           + p[:, 3 * dm + hd * LANES:3 * dm + (hd + 1) * LANES] * sk).astype(BF16)
    vt_ref[0] = _nt(wvt_ref[...], hb).astype(BF16)


def _gla_proj_body(has_prev, *refs):
    x, h, rest = _prologue(refs, has_prev, 0)
    (w_ref, wvt_ref, wgu_ref, bg_ref), outs = rest[:4], rest[4:]
    if has_prev:
        x1_ref, q_ref, k_ref, v_ref, vt_ref, g_ref, af_ref, ab_ref = outs
        x1_ref[0] = x
    else:
        q_ref, k_ref, v_ref, vt_ref, g_ref, af_ref, ab_ref = outs
    dm = h.shape[-1]
    dk = dm // 2
    hb = h.astype(BF16)
    p = _mm(hb, w_ref[...])
    q_ref[0] = p[:, :dk] * ((dk // GLA_HEADS) ** -0.5)
    k_ref[0] = p[:, dk:2 * dk]
    v_ref[0] = p[:, 2 * dk:2 * dk + dm].astype(BF16)
    vt_ref[0] = _nt(wvt_ref[...], hb).astype(BF16)
    g_ref[0] = p[:, 2 * dk + dm:2 * dk + 2 * dm]
    z = p[:, 2 * dk + 2 * dm:].astype(BF16)
    a = _mm(z, wgu_ref[...]) + bg_ref[...]
    ls = (jnp.minimum(a, 0.0) - jnp.log(1.0 + jnp.exp(-jnp.abs(a)))) / GLA_GATE_NORM
    af_ref[0] = ls[:, :dk]
    ab_ref[0] = ls[:, dk:]


def _route(logits_t):
    def col(i):
        return logits_t[i:i + 1, :]

    lg = [col(i) for i in range(MOE_GROUPS)]
    gmax = functools.reduce(jnp.maximum, lg)
    onehot, taken = [], None
    for i in range(MOE_GROUPS):
        hit = lg[i] == gmax
        if taken is not None:
            hit = jnp.logical_and(hit, jnp.logical_not(taken))
        taken = hit if taken is None else jnp.logical_or(taken, hit)
        onehot.append(hit)
    grp_w = 1.0 / functools.reduce(lambda a, b: a + b, [jnp.exp(l - gmax) for l in lg])
    le = []
    for e in range(MOE_EXPERTS):
        v = None
        for g in range(MOE_GROUPS):
            c = jnp.where(onehot[g], col(MOE_GROUPS + g * MOE_EXPERTS + e), 0.0)
            v = c if v is None else v + c
        le.append(v)
    vmax = functools.reduce(jnp.maximum, le)
    num = []
    for e in range(MOE_EXPERTS):
        rank = None
        for o in range(MOE_EXPERTS):
            if o == e:
                continue
            ahead = (le[o] >= le[e]) if o < e else (le[o] > le[e])
            r = jnp.where(ahead, 1.0, 0.0)
            rank = r if rank is None else rank + r
        num.append(jnp.where(rank < 2.0, jnp.exp(le[e] - vmax), 0.0))
    scale = grp_w / functools.reduce(lambda a, b: a + b, num)
    return [jnp.where(onehot[g], num[e] * scale, 0.0) for g in range(MOE_GROUPS) for e in range(MOE_EXPERTS)]


def _post_body(kind, *refs):
    x1_ref = refs[0]
    if kind == "gla":
        of_ref, ob_ref, g_ref, hn_ref = refs[1:5]
        rest = refs[5:]
        o = of_ref[0] + ob_ref[0]
        hn = hn_ref[...]
        dv = hn.shape[-1]
        parts = [_rms(o[:, i * dv:(i + 1) * dv]) * hn for i in range(o.shape[-1] // dv)]
        a = (jnp.concatenate(parts, axis=-1) * _silu(g_ref[0])).astype(BF16)
    else:
        a = refs[1][0]
        rest = refs[2:]
    mod_ref, gain_ref, wo_ref, wr_ref, br_ref, x2_ref, h2_ref, gate_ref = rest
    x2 = x1_ref[0] + mod_ref[0, 2:3, :] * _mm(a, wo_ref[...])
    x2_ref[0] = x2
    h2 = _rms(x2) * gain_ref[...]
    h2 = (h2 * (1.0 + mod_ref[0, 4:5, :]) + mod_ref[0, 3:4, :]).astype(BF16)
    h2_ref[0] = h2
    for i, row in enumerate(_route((_mm(h2, wr_ref[...]) + br_ref[...]).T)):
        gate_ref[i:i + 1, :] = row


def _final_body(x_ref, y_ref, mod_ref, gain_ref, o_ref):
    x = x_ref[0] + mod_ref[0, 5:6, :] * y_ref[0]
    o_ref[0] = _rms(x) * gain_ref[...]


def _attn_body(mode, latent, n_lat, n_ctx, tk, lam_init, *refs):
    if not latent:
        refs = refs[1:]
    if mode == "mla":
        q_ref, k_ref, vt_ref, o_ref, s_sc, m_sc, l_sc, acc_sc = refs
    else:
        q_ref, k_ref, vt_ref, lam_ref, sub_ref, o_ref, s_sc, m_sc, l_sc, acc_sc = refs
    dv = acc_sc.shape[1]
    n_steps = (n_lat + n_ctx) // tk

    def q_of(i):
        if mode == "mla":
            return q_ref[0, :, i * LANES:(i + 1) * LANES]
        q = q_ref[0]
        lane = lax.broadcasted_iota(jnp.int32, q.shape, 1)
        keep = (lane < DIFF_HEAD_DIM) if i == 0 else (lane >= DIFF_HEAD_DIM)
        return jnp.where(keep, q, jnp.zeros_like(q))

    qs = [q_of(0), q_of(1)]

    def scores(i, rows):
        k = k_ref[0, rows, i * LANES:(i + 1) * LANES] if mode == "mla" else k_ref[0, rows, :]
        return _nt(k, qs[i])

    def vt_of(i, cols):
        return vt_ref[0, i * dv:(i + 1) * dv, cols] if mode == "mla" else vt_ref[0, :, cols]

    def absorb(i, s, cols, first=False):
        m_new = jnp.max(s, axis=0, keepdims=True)
        if not first:
            m_old = m_sc[i]
            m_new = jnp.maximum(m_old, m_new)
            alpha = jnp.exp2(m_old - m_new)
        p = jnp.exp2(s - m_new)
        l = jnp.sum(p, axis=0, keepdims=True)
        pv = _mm(vt_of(i, cols), p.astype(BF16))
        l_sc[i] = l if first else alpha * l_sc[i] + l
        acc_sc[i] = pv if first else alpha * acc_sc[i] + pv
        m_sc[i] = m_new

    if not latent:
        ctx = slice(n_lat, n_lat + n_ctx)
        for i in range(2):
            absorb(i, scores(i, ctx), ctx, first=True)
    else:
        for i in range(2):
            s_sc[0, i] = scores(i, slice(0, tk))
        for c in range(n_steps):
            slot = c % 2
            for i in range(2):
                if c + 1 < n_steps:
                    s_next = scores(i, slice((c + 1) * tk, (c + 2) * tk))
                absorb(i, s_sc[slot, i], slice(c * tk, (c + 1) * tk), first=(c == 0))
                if c + 1 < n_steps:
                    s_sc[1 - slot, i] = s_next

    o0 = acc_sc[0] * (1.0 / l_sc[0])
    o1 = acc_sc[1] * (1.0 / l_sc[1])
    if mode == "mla":
        o = jnp.concatenate([o0, o1], axis=0)
    else:
        lam = lam_ref[...]
        lam_full = (jnp.exp(jnp.sum(lam[0:1] * lam[1:2], axis=-1, keepdims=True))
                    - jnp.exp(jnp.sum(lam[2:3] * lam[3:4], axis=-1, keepdims=True)) + lam_init)
        o = o0 - lam_full * o1
        o = o * lax.rsqrt(jnp.mean(o * o, axis=0, keepdims=True) + EPS) * sub_ref[...] * (1.0 - lam_init)
    o_ref[0] = o.T.astype(BF16)


def _attention(mode, q, k, vt, n_lat, n_ctx, extra=(), lam_init=0.0):
    B, T, _ = q.shape
    tk = max(t for t in range(LANES, ATTN_KEYS + 1, LANES) if T % t == 0)
    qw = 2 * LANES if mode == "mla" else LANES
    dv = MLA_V if mode == "mla" else LANES
    nh = q.shape[-1] // qw
    kv_specs = [pl.BlockSpec((1, T, qw), lambda b, h, i: (b, 0, h)),
                pl.BlockSpec((1, LANES, T), lambda b, h, i: (b, h, 0))]
    for a in extra:
        kv_specs.append(pl.BlockSpec(a.shape, lambda b, h, i, nd=a.ndim: (0,) * nd))
    out_shape = jax.ShapeDtypeStruct((B, T, nh * LANES), BF16)

    def call(latent, tq, n_tiles, first_tile, keys, ins, in_specs, **kw):
        return pl.pallas_call(
            functools.partial(_attn_body, mode, latent, n_lat, n_ctx, tk, lam_init),
            grid=(B, nh, n_tiles),
            in_specs=in_specs + [pl.BlockSpec((1, tq, qw), lambda b, h, i: (b, first_tile + i, h))] + kv_specs,
            out_specs=pl.BlockSpec((1, tq, LANES), lambda b, h, i: (b, first_tile + i, h)),
            out_shape=out_shape,
            scratch_shapes=[pltpu.VMEM((2, 2, keys, tq), F32), pltpu.VMEM((2, 1, tq), F32),
                            pltpu.VMEM((2, 1, tq), F32), pltpu.VMEM((2, dv, tq), F32)],
            compiler_params=_cp("parallel", "parallel", "arbitrary"),
            name="attn_" + mode + ("" if latent else "_ctx"), **kw,
        )(*ins, q, k, vt, *extra)

    tq = math.gcd(n_lat, ATTN_Q)
    o = call(True, tq, n_lat // tq, 0, tk, [], [])
    tc = math.gcd(math.gcd(n_lat, n_ctx), ATTN_Q)
    return call(False, tc, n_ctx // tc, n_lat // tc, 8, [o], [pl.BlockSpec(memory_space=pl.ANY)],
                input_output_aliases={0: 0})


def _cumsum_rows(tri, x):
    hi = x.astype(BF16)
    r1 = x - hi.astype(F32)
    mid = r1.astype(BF16)
    lo = (r1 - mid.astype(F32)).astype(BF16)
    return _mm(tri, hi) + _mm(tri, mid) + _mm(tri, lo)


def _gla_scan_body(n_chunks, *refs):
    (qf_ref, kf_ref, vf_ref, vtf_ref, af_ref, qb_ref, kb_ref, vb_ref, vtb_ref, ab_ref,
     of_ref, ob_ref, sf_sc, sb_sc) = refs
    C = GLA_CHUNK
    n_heads, dvh, dkh = sf_sc.shape

    @pl.when(pl.program_id(1) == 0)
    def _():
        sf_sc[...] = jnp.zeros_like(sf_sc)
        sb_sc[...] = jnp.zeros_like(sb_sc)

    r = lax.broadcasted_iota(jnp.int32, (C, C), 0)
    cidx = lax.broadcasted_iota(jnp.int32, (C, C), 1)
    lower = cidx <= r

    def chunk(ci, hd, q_ref, k_ref, v_ref, vt_ref, a_ref, o_ref, s_sc, reverse):
        rows = slice(ci * C, (ci + 1) * C)
        kl = slice(hd * dkh, (hd + 1) * dkh)
        vl = slice(hd * dvh, (hd + 1) * dvh)
        keep = jnp.logical_not(lower) | (cidx == r) if reverse else lower
        tri = jnp.where(keep, 1.0, 0.0).astype(BF16)
        cum = _cumsum_rows(tri, a_ref[0, rows, kl])
        total = cum[0:1, :] if reverse else cum[C - 1:C, :]
        q = q_ref[0, rows, kl]
        k = k_ref[0, rows, kl]
        qd = (q * jnp.exp(cum)).astype(BF16)
        ki = (k * jnp.exp(-cum)).astype(BF16)
        kd = (k * jnp.exp(total - cum)).astype(BF16)
        att = jnp.where(keep, _nt(qd, ki), 0.0).astype(BF16)
        st = s_sc[hd]
        o_ref[0, rows, vl] = _nt(qd, st.astype(BF16)) + _mm(att, v_ref[0, rows, vl])
        s_sc[hd] = st * jnp.exp(total) + _mm(vt_ref[0, vl, rows], kd)

    for ci in range(n_chunks):
        for hd in range(n_heads):
            chunk(ci, hd, qf_ref, kf_ref, vf_ref, vtf_ref, af_ref, of_ref, sf_sc, False)
            chunk(n_chunks - 1 - ci, hd, qb_ref, kb_ref, vb_ref, vtb_ref, ab_ref, ob_ref, sb_sc, True)


def _gla_scan(q, k, v, vt, af, ab, n_lat):
    B, T, dk = q.shape
    dm = v.shape[-1]
    dkh, dvh = dk // GLA_HEADS, dm // GLA_HEADS
    lb = 2 * GLA_CHUNK
    nb = T // lb
    n_lat_b = n_lat // lb
    n_ctx_b = nb - n_lat_b

    def fwd(j):
        return jnp.where(j < n_ctx_b, n_lat_b + j, j - n_ctx_b)

    def bwd(j):
        return jnp.where(j < n_ctx_b, nb - 1 - j, nb - 1 - j)

    def specs(blk):
        return [pl.BlockSpec((1, lb, dk), lambda b, j: (b, blk(j), 0)),
                pl.BlockSpec((1, lb, dk), lambda b, j: (b, blk(j), 0)),
                pl.BlockSpec((1, lb, dm), lambda b, j: (b, blk(j), 0)),
                pl.BlockSpec((1, dm, lb), lambda b, j: (b, 0, blk(j))),
                pl.BlockSpec((1, lb, dk), lambda b, j: (b, blk(j), 0))]

    o_sds = jax.ShapeDtypeStruct((B, T, dm), F32)
    state = pltpu.VMEM((GLA_HEADS, dvh, dkh), F32)
    return pl.pallas_call(
        functools.partial(_gla_scan_body, lb // GLA_CHUNK),
        grid=(B, nb),
        in_specs=specs(fwd) + specs(bwd),
        out_specs=[pl.BlockSpec((1, lb, dm), lambda b, j: (b, fwd(j), 0)),
                   pl.BlockSpec((1, lb, dm), lambda b, j: (b, bwd(j), 0))],
        out_shape=[o_sds, o_sds],
        scratch_shapes=[state, state],
        compiler_params=_cp("parallel", "arbitrary"),
        name="gla_scan",
    )(q, k, v, vt, af, q, k, v, vt, ab)


def _moe_body(ne, h_ref, g_ref, tri_ref, wg_ref, wu_ref, wdt_ref, y_ref, key_sc, gt_sc, yt_sc):
    e = pl.program_id(1)
    tm = h_ref.shape[0]
    C = MOE_SLOTS

    @pl.when(e == 0)
    def _():
        gt = g_ref[...]
        sel = gt != 0.0
        rank = _mm(jnp.where(sel, 1.0, 0.0).astype(BF16), tri_ref[...])
        key_sc[...] = jnp.where(sel, rank, -1.0)
        gt_sc[...] = gt
        yt_sc[...] = jnp.zeros_like(yt_sc)

    keys = [key_sc[pl.ds(e * MOE_PAIR + k, 1), :] for k in range(MOE_PAIR)]
    gates = [gt_sc[pl.ds(e * MOE_PAIR + k, 1), :] for k in range(MOE_PAIR)]

    def select(k, r):
        slot = (lax.broadcasted_iota(jnp.int32, (C, tm), 0) + r * C).astype(F32)
        hit = keys[k] == slot
        return jnp.where(hit, 1.0, 0.0).astype(BF16), jnp.where(hit, gates[k], 0.0).astype(BF16)

    def expert(k, xg):
        a = _mm(xg, wg_ref[k])
        u = _mm(xg, wu_ref[k])
        act = (_silu(a) * u).astype(BF16)
        return _nt(wdt_ref[k], act).astype(BF16)

    def run(ks, r):
        sel = [select(k, r) for k in ks]
        xg = _mm(jnp.concatenate([p for p, _ in sel], axis=0), h_ref[...]).astype(BF16)
        zt = [expert(k, xg[i * C:(i + 1) * C]) for i, k in enumerate(ks)]
        yt_sc[...] += _mm(jnp.concatenate(zt, axis=1), jnp.concatenate([pg for _, pg in sel], axis=0))

    run(list(range(MOE_PAIR)), 0)
    for k in range(MOE_PAIR):
        n_chunks = (jnp.max(keys[k]).astype(jnp.int32) + C) // C

        def more(r, carry, k=k):
            run([k], r)
            return carry

        lax.fori_loop(1, n_chunks, more, 0)

    @pl.when(e == ne // MOE_PAIR - 1)
    def _():
        y_ref[...] = yt_sc[...].T


def _moe(h2, gates, wg, wu, wdt):
    n, d = h2.shape
    ne, _, f = wg.shape
    tm = math.gcd(n, MOE_TILE)
    tri = jnp.triu(jnp.ones((tm, tm), F32), 1).astype(BF16)
    return pl.pallas_call(
        functools.partial(_moe_body, ne),
        grid=(n // tm, ne // MOE_PAIR),
        in_specs=[pl.BlockSpec((tm, d), lambda t, e: (t, 0)),
                  pl.BlockSpec((ne, tm), lambda t, e: (0, t)),
                  pl.BlockSpec((tm, tm), lambda t, e: (0, 0)),
                  pl.BlockSpec((MOE_PAIR, d, f), lambda t, e: (e, 0, 0)),
                  pl.BlockSpec((MOE_PAIR, d, f), lambda t, e: (e, 0, 0)),
                  pl.BlockSpec((MOE_PAIR, d, f), lambda t, e: (e, 0, 0))],
        out_specs=pl.BlockSpec((tm, d), lambda t, e: (t, 0)),
        out_shape=jax.ShapeDtypeStruct((n, d), F32),
        scratch_shapes=[pltpu.VMEM((ne, tm), F32), pltpu.VMEM((ne, tm), F32), pltpu.VMEM((d, tm), F32)],
        compiler_params=_cp("parallel", "arbitrary"),
        name="moe",
    )(h2, gates, tri, wg, wu, wdt)


def _rope_angles(n_lat, n_ctx, rot_dim):
    rows = n_lat // GRID_W
    row = jnp.repeat(jnp.arange(rows, dtype=F32), GRID_W)
    col = jnp.tile(jnp.arange(GRID_W, dtype=F32), rows)
    n_freq = rot_dim // 4
    inv_freq = ROPE_BASE ** (-jnp.arange(n_freq, dtype=F32) / n_freq)
    ang = jnp.concatenate([row[:, None] * inv_freq, col[:, None] * inv_freq], axis=-1)
    cos = jnp.concatenate([jnp.cos(ang), jnp.ones((n_ctx, rot_dim // 2), F32)], axis=0)
    sin = jnp.concatenate([jnp.sin(ang), jnp.zeros((n_ctx, rot_dim // 2), F32)], axis=0)
    return cos, sin


def _swap_halves(w, width):
    lead = w.shape[:-1]
    w = w.reshape(*lead, -1, 2, width // 2)
    return jnp.flip(w, axis=-2).reshape(*lead, -1)


def kernel(x, c, ctx, c_ctx, w_ada, b_ada, norm_mix, norm_ffn, mla_w_dqkv, mla_q_norm, mla_w_uq, mla_kv_norm, mla_w_ukv, mla_w_o, diff_w_qkv, diff_lambda, diff_subln, diff_w_o, gla_w_in, gla_w_gate_up, gla_b_gate, gla_head_norm, gla_w_o, moe_w_group, moe_b_group, moe_w_router, moe_b_router, moe_w_gate, moe_w_up, moe_w_down, final_norm):
    B, n_lat, D = x.shape
    n_ctx = ctx.shape[1]
    T = n_lat + n_ctx
    depth = w_ada.shape[0]
    assert n_lat % ROW_TILE == 0 and n_ctx % ROW_TILE == 0 and B + 1 <= 8 and D % LANES == 0
    R = _Rows(B, T, n_lat)

    cvec = jnp.zeros((8, D), F32).at[:B].set(c).at[B].set(c_ctx)
    mods = _ada_all(cvec, w_ada, b_ada).reshape(depth, 8, N_MOD, D)

    stream = jnp.concatenate([x, ctx], axis=1)
    y_prev = None

    cos_a, sin_a = _rope_angles(n_lat, n_ctx, MLA_ROPE)
    ones, zeros = jnp.ones((T, MLA_NOPE), F32), jnp.zeros((T, LANES - MLA_NOPE - MLA_ROPE), F32)
    mla_cos = jnp.concatenate([ones, cos_a, cos_a, zeros], axis=1)
    mla_sin = jnp.concatenate([0 * ones, -sin_a, sin_a, zeros], axis=1)
    cos_b, sin_b = _rope_angles(n_lat, n_ctx, DIFF_HEAD_DIM)
    diff_cos = jnp.concatenate([cos_b] * 4, axis=1)
    diff_sin = jnp.concatenate([-sin_b, sin_b] * 2, axis=1)
    diff_scale = DIFF_HEAD_DIM ** -0.5 * math.log2(math.e)
    mla_qscale = (MLA_NOPE + MLA_ROPE) ** -0.5 * math.log2(math.e)

    for li in range(depth):
        last = li == depth - 1
        kind, j = li % 3, li // 3
        mod = mods[li]
        has_prev = y_prev is not None
        head = [stream] + ([y_prev, mods[li - 1]] if has_prev else []) + [mod, norm_mix[li].reshape(1, D)]
        head_specs = [R.rows(D)] + ([R.rows(D), R.mod(D)] if has_prev else []) + [R.mod(D), R.full(head[-1])]
        x1_out = ([R.out(D, F32)], [R.rows(D)]) if has_prev else ([], [])

        if kind == 0:
            wd = mla_w_dqkv[j]
            pe_w = wd[:, MLA_Q_RANK + MLA_KV_RANK:]
            padl, padr = jnp.zeros((D, MLA_NOPE), F32), jnp.zeros((D, LANES - MLA_NOPE - MLA_ROPE), F32)
            wdown = jnp.concatenate([wd[:, :MLA_Q_RANK + MLA_KV_RANK], padl, pe_w, padr,
                                     padl, _swap_halves(pe_w, MLA_ROPE), padr], axis=1).astype(BF16)
            wq = mla_w_uq[j].reshape(MLA_Q_RANK, MLA_HEADS, MLA_NOPE + MLA_ROPE)
            pad = jnp.zeros((MLA_Q_RANK, MLA_HEADS, LANES - MLA_NOPE - MLA_ROPE), F32)
            wq_a = jnp.concatenate([wq, pad], axis=-1).reshape(MLA_Q_RANK, -1).astype(BF16)
            wq_s = jnp.concatenate([0 * wq[..., :MLA_NOPE], _swap_halves(wq[..., MLA_NOPE:], MLA_ROPE), pad],
                                   axis=-1).reshape(MLA_Q_RANK, -1).astype(BF16)
            wkv = mla_w_ukv[j].reshape(MLA_KV_RANK, MLA_HEADS, MLA_NOPE + MLA_V)
            wk = jnp.concatenate([wkv[..., :MLA_NOPE], jnp.zeros((MLA_KV_RANK, MLA_HEADS, LANES - MLA_NOPE), F32)],
                                 axis=-1).reshape(MLA_KV_RANK, -1).astype(BF16)
            wvt = wkv[..., MLA_NOPE:].reshape(MLA_KV_RANK, -1).T.astype(BF16)
            ins = head + [wdown, mla_q_norm[j].reshape(1, -1), mla_kv_norm[j].reshape(1, -1),
                          wq_a, wq_s, wk, wvt, mla_cos * mla_qscale, mla_sin * mla_qscale, mla_cos, mla_sin]
            specs = head_specs + [R.full(a) for a in ins[len(head):-4]] + [R.table(LANES)] * 4
            outs = pl.pallas_call(
                functools.partial(_mla_proj_body, has_prev), grid=R.grid, in_specs=specs,
                out_specs=x1_out[1] + [R.rows(MLA_HEADS * LANES), R.rows(MLA_HEADS * LANES), R.cols(MLA_HEADS * MLA_V)],
                out_shape=x1_out[0] + [R.out(MLA_HEADS * LANES, BF16), R.out(MLA_HEADS * LANES, BF16),
                                       jax.ShapeDtypeStruct((B, MLA_HEADS * MLA_V, T), BF16)],
                compiler_params=_cp("parallel", "parallel"), name="mla_proj")(*ins)
            if has_prev:
                stream = outs[0]
            q, k, vt = outs[-3:]
            mix = [_attention("mla", q, k, vt, n_lat, n_ctx)]
            mix_specs = [R.rows(D)]
            w_o = mla_w_o[j]
        elif kind == 1:
            lam_init = 0.8 - 0.6 * math.exp(-0.3 * li)
            wq, wk, wv = jnp.split(diff_w_qkv[j], 3, axis=1)
            w = jnp.concatenate([wq, wk, _swap_halves(wq, DIFF_HEAD_DIM), _swap_halves(wk, DIFF_HEAD_DIM)],
                                axis=1).astype(BF16)
            wvt = wv.T.astype(BF16)
            ins = head + [w, wvt, diff_cos * diff_scale, diff_sin * diff_scale, diff_cos, diff_sin]
            specs = head_specs + [R.full(w), R.full(wvt)] + [R.table(LANES)] * 4
            outs = pl.pallas_call(
                functools.partial(_diff_proj_body, has_prev), grid=R.grid, in_specs=specs,
                out_specs=x1_out[1] + [R.rows(D), R.rows(D), R.cols(D)],
                out_shape=x1_out[0] + [R.out(D, BF16), R.out(D, BF16), jax.ShapeDtypeStruct((B, D, T), BF16)],
                compiler_params=_cp("parallel", "parallel"), name="diff_proj")(*ins)
            if has_prev:
                stream = outs[0]
            q, k, vt = outs[-3:]
            mix = [_attention("diff", q, k, vt, n_lat, n_ctx,
                              extra=(diff_lambda[j], diff_subln[j].reshape(-1, 1)), lam_init=lam_init)]
            mix_specs = [R.rows(D)]
            w_o = diff_w_o[j]
        else:
            dk = D // 2
            win = gla_w_in[j]
            zpad = jnp.zeros((D, LANES - 2 * GLA_GATE_RANK), F32)
            w = jnp.concatenate([win, zpad], axis=1).astype(BF16)
            wvt = win[:, 2 * dk:2 * dk + D].T.astype(BF16)
            wgu = jnp.zeros((LANES, 2 * dk), F32)
            wgu = wgu.at[:GLA_GATE_RANK, :dk].set(gla_w_gate_up[j, 0])
            wgu = wgu.at[GLA_GATE_RANK:2 * GLA_GATE_RANK, dk:].set(gla_w_gate_up[j, 1]).astype(BF16)
            bg = gla_b_gate[j].reshape(1, 2 * dk)
            ins = head + [w, wvt, wgu, bg]
            specs = head_specs + [R.full(a) for a in ins[len(head):]]
            vt_spec = pl.BlockSpec((1, D, R.tm), lambda b, t: (b, 0, t))
            outs = pl.pallas_call(
                functools.partial(_gla_proj_body, has_prev), grid=R.grid, in_specs=specs,
                out_specs=x1_out[1] + [R.rows(dk), R.rows(dk), R.rows(D), vt_spec, R.rows(D), R.rows(dk), R.rows(dk)],
                out_shape=x1_out[0] + [R.out(dk, F32), R.out(dk, F32), R.out(D, BF16),
                                       jax.ShapeDtypeStruct((B, D, T), BF16), R.out(D, F32),
                                       R.out(dk, F32), R.out(dk, F32)],
                compiler_params=_cp("parallel", "parallel"), name="gla_proj")(*ins)
            if has_prev:
                stream = outs[0]
            q, k, v, vt, g, af, ab = outs[-7:]
            o_f, o_b = _gla_scan(q, k, v, vt, af, ab, n_lat)
            hn = gla_head_norm[j].reshape(1, -1)
            mix = [o_f, o_b, g, hn]
            mix_specs = [R.rows(D), R.rows(D), R.rows(D), R.full(hn)]
            w_o = gla_w_o[j]

        wr = jnp.concatenate([moe_w_group[li], jnp.moveaxis(moe_w_router[li], 0, 1).reshape(D, -1)], axis=1)
        n_r = wr.shape[1]
        wr = jnp.concatenate([wr, jnp.zeros((D, LANES - n_r), F32)], axis=1).astype(BF16)
        br = jnp.concatenate([moe_b_group[li], moe_b_router[li].reshape(-1), jnp.zeros((LANES - n_r,), F32)]).reshape(1, LANES)
        tail = [mod, norm_ffn[li].reshape(1, D), w_o.astype(BF16), wr, br]
        ne = MOE_GROUPS * MOE_EXPERTS
        nt = T // R.tm
        stream, h2, gates_t = pl.pallas_call(
            functools.partial(_post_body, "gla" if kind == 2 else "attn"), grid=R.grid,
            in_specs=[R.rows(D)] + mix_specs + [R.mod(D)] + [R.full(a) for a in tail[1:]],
            out_specs=[R.rows(D), R.rows(D), pl.BlockSpec((ne, R.tm), lambda b, t: (0, b * nt + t))],
            out_shape=[R.out(D, F32), R.out(D, BF16), jax.ShapeDtypeStruct((ne, B * T), F32)],
            compiler_params=_cp("parallel", "parallel"), name="post_mixer")(stream, *mix, *tail)

        f = moe_w_gate.shape[-1]
        y = _moe(h2.reshape(B * T, D), gates_t,
                 moe_w_gate[li].reshape(ne, D, f).astype(BF16), moe_w_up[li].reshape(ne, D, f).astype(BF16),
                 jnp.swapaxes(moe_w_down[li].reshape(ne, f, D), 1, 2).astype(BF16))
        y_prev = y.reshape(B, T, D)

    RL = _Rows(B, n_lat, n_lat)
    fin = final_norm.reshape(1, D)
    return pl.pallas_call(
        _final_body, grid=RL.grid,
        in_specs=[RL.rows(D), RL.rows(D), RL.mod(D), RL.full(fin)],
        out_specs=RL.rows(D), out_shape=RL.out(D, F32),
        compiler_params=_cp("parallel", "parallel"), name="final_norm")(stream, y_prev, mods[depth - 1], fin)
```
